```python
import math
import jax
import jax.numpy as jnp
from jax import lax
import numpy as np

D_MODEL = 4096
BATCH = 1
SEQ = 8192
DEPTH = 1
DEC_BATCH = 128
DEC_SEQ = 1
PAST_LEN = 2048
PAGE_SIZE = 128

ATTN_HEADS = 8
ATTN_HEAD_DIM = 128
ATTN_V_DIM = 2 * ATTN_HEAD_DIM
ATTN_QK_WIDTH = ATTN_HEADS * 2 * ATTN_HEAD_DIM
ATTN_WIDTH = ATTN_HEADS * ATTN_V_DIM
Q_BLOCK = 128
SSM_WIDTH = D_MODEL - ATTN_WIDTH
SSM_HEAD_DIM = 64
SSM_HEADS = SSM_WIDTH // SSM_HEAD_DIM
SSM_GROUPS = 8
SSM_HEADS_PER_GROUP = SSM_HEADS // SSM_GROUPS
SSM_STATE = 128
SSM_CONV = 4
SSM_CHUNK = 128
SSM_CONV_DIM = SSM_WIDTH + 2 * SSM_GROUPS * SSM_STATE
IN_PROJ_DIM = 2 * ATTN_QK_WIDTH + ATTN_WIDTH + SSM_WIDTH + SSM_CONV_DIM + SSM_HEADS
PEER_HEADS = 8
PEER_KEYS = 128
PEER_EXPERTS = PEER_KEYS * PEER_KEYS
PEER_KEY_DIM = 256
PEER_HALF = PEER_KEY_DIM // 2
PEER_TOPK = 16
PEER_TOKEN_BLOCK = 64
NORM_EPS = 1e-6

kernel_name = 'hymba_diffattn_mamba2_peer_step'


def rms_norm(x, g):
    xf = x.astype(jnp.float32)
    y = xf * lax.rsqrt(jnp.mean(xf * xf, axis=-1, keepdims=True) + NORM_EPS)
    return (y * g.astype(jnp.float32)).astype(x.dtype)


def split_points():
    sizes = [ATTN_QK_WIDTH, ATTN_QK_WIDTH, ATTN_WIDTH, SSM_WIDTH, SSM_CONV_DIM, SSM_HEADS]
    return [int(s) for s in np.cumsum(sizes)[:-1]]


def alibi_slopes():
    return 2.0 ** (-8.0 * jnp.arange(1, ATTN_HEADS + 1, dtype=jnp.float32) / ATTN_HEADS)


def diff_attend(q, k, v, q_pos, k_pos, lam):
    s = jnp.einsum('bqhjd,bkhjd->bhjqk', q, k).astype(jnp.float32) * (ATTN_HEAD_DIM ** -0.5)
    dist = (q_pos[:, None] - k_pos[None, :]).astype(jnp.float32)
    s = s - alibi_slopes()[None, :, None, None, None] * dist
    s = jnp.where(dist >= 0, s, -jnp.inf)
    p = jax.nn.softmax(s, axis=-1)
    w = (p[:, :, 0] - lam * p[:, :, 1]).astype(v.dtype)
    return jnp.einsum('bhqk,bkhd->bqhd', w, v)


def prompt_attention(q, k, v, lam):
    b, l = q.shape[:2]
    qb = min(Q_BLOCK, l)
    nb = l // qb
    k_pos = jnp.arange(l)
    q_blocks = q.reshape(b, nb, qb, ATTN_HEADS, 2, ATTN_HEAD_DIM).swapaxes(0, 1)

    def one_block(args):
        i, q_blk = args
        q_pos = i * qb + jnp.arange(qb)
        return diff_attend(q_blk, k, v, q_pos, k_pos, lam)

    out = lax.map(one_block, (jnp.arange(nb), q_blocks))
    return out.swapaxes(0, 1).reshape(b, l, ATTN_HEADS, ATTN_V_DIM)


def sample_attention(q, k_new, v_new, cache_k, cache_v, page_table, lam):
    db, ds = q.shape[:2]
    past = page_table.shape[1] * cache_k.shape[1]
    k_past = cache_k[page_table].reshape(db, past, ATTN_HEADS, 2, ATTN_HEAD_DIM)
    v_past = cache_v[page_table].reshape(db, past, ATTN_HEADS, ATTN_V_DIM)
    k = jnp.concatenate([k_past.astype(k_new.dtype), k_new], axis=1)
    v = jnp.concatenate([v_past.astype(v_new.dtype), v_new], axis=1)
    q_pos = past + jnp.arange(ds)
    k_pos = jnp.arange(past + ds)
    return diff_attend(q, k, v, q_pos, k_pos, lam)


def ssd_scan(xdt, a, bm, cm, h0):
    G, R, P, N = SSM_GROUPS, SSM_HEADS_PER_GROUP, SSM_HEAD_DIM, SSM_STATE
    b, l = xdt.shape[:2]
    q = min(SSM_CHUNK, l)
    nc = -(-l // q)
    pad = nc * q - l

    def padf(t):
        return jnp.pad(t, [(0, 0), (0, pad)] + [(0, 0)] * (t.ndim - 2))

    xdt = padf(xdt).reshape(b, nc, q, G, R, P)
    a = padf(a).reshape(b, nc, q, G, R).transpose(0, 3, 4, 1, 2)
    bm = padf(bm).reshape(b, nc, q, G, N)
    cm = padf(cm).reshape(b, nc, q, G, N)
    a_cum = jnp.cumsum(a, axis=-1)
    causal = jnp.tril(jnp.ones((q, q), dtype=bool))
    seg = a_cum[..., :, None] - a_cum[..., None, :]
    decay_in = jnp.exp(jnp.where(causal, seg, -jnp.inf))
    cb = jnp.einsum('bclgn,bcsgn->bgcls', cm, bm)
    y_diag = jnp.einsum('bgcls,bgrcls,bcsgrp->bclgrp', cb, decay_in, xdt)
    decay_states = jnp.exp(a_cum[..., -1:] - a_cum)
    states = jnp.einsum('bcsgn,bgrcs,bcsgrp->cbgrpn', bm, decay_states, xdt)
    chunk_decay = jnp.exp(a_cum[..., -1]).transpose(3, 0, 1, 2)

    def step(h, inp):
        dec, st = inp
        return dec[..., None, None] * h + st, h

    h_final, h_in = lax.scan(step, h0.reshape(b, G, R, P, N).astype(jnp.float32),
                             (chunk_decay, states))
    y_off = jnp.einsum('bclgn,cbgrpn,bgrcl->bclgrp', cm, h_in, jnp.exp(a_cum))
    y = (y_diag + y_off).reshape(b, nc * q, SSM_HEADS, P)[:, :l]
    return y, h_final.reshape(b, SSM_HEADS, P, N)


def peer_ffn(xn, peer_wq, peer_keys, peer_u, peer_v):
    b, l, dm = xn.shape
    t = b * l
    xf = xn.reshape(t, dm)
    q = (xf @ peer_wq).reshape(t, PEER_HEADS, 2, PEER_HALF)
    s = jnp.einsum('thjd,hjnd->thjn', q, peer_keys).astype(jnp.float32)
    s1, i1 = lax.top_k(s[:, :, 0], PEER_TOPK)
    s2, i2 = lax.top_k(s[:, :, 1], PEER_TOPK)
    cand = (s1[..., :, None] + s2[..., None, :]).reshape(t, PEER_HEADS, PEER_TOPK * PEER_TOPK)
    cidx = (i1[..., :, None] * PEER_KEYS + i2[..., None, :]).reshape(t, PEER_HEADS, PEER_TOPK * PEER_TOPK)
    top, pos = lax.top_k(cand, PEER_TOPK)
    idx = jnp.take_along_axis(cidx, pos, axis=-1).reshape(t, PEER_HEADS * PEER_TOPK)
    gate = jax.nn.softmax(top, axis=-1).reshape(t, PEER_HEADS * PEER_TOPK)
    tb = min(PEER_TOKEN_BLOCK, t)
    nb = -(-t // tb)
    pad = nb * tb - t
    xp = jnp.pad(xf, [(0, pad), (0, 0)]).reshape(nb, tb, dm)
    ip = jnp.pad(idx, [(0, pad), (0, 0)]).reshape(nb, tb, -1)
    gp = jnp.pad(gate, [(0, pad), (0, 0)]).reshape(nb, tb, -1)

    def block(args):
        xb, ib, gb = args
        act = jnp.einsum('td,tkd->tk', xb, peer_u[ib]).astype(jnp.float32)
        hb = (jax.nn.gelu(act, approximate=False) * gb).astype(xb.dtype)
        return jnp.einsum('tk,tkd->td', hb, peer_v[ib])

    out = lax.map(block, (xp, ip, gp))
    return out.reshape(nb * tb, dm)[:t].reshape(b, l, dm)


def hybrid_layer(x, conv_buf, h0, attend_fn, layer_idx, g_mix, w_in, lambda_q, lambda_k,
                 subln_g, conv_w, conv_b, dt_bias, a_log, d_skip, ssm_norm_g, w_out,
                 g_ffn, peer_wq, peer_keys, peer_u, peer_v):
    f32 = jnp.float32
    b, l, _ = x.shape
    xn = rms_norm(x, g_mix)
    q, k, v, z, xbc, dt_raw = jnp.split(xn @ w_in, split_points(), axis=-1)
    q = q.reshape(b, l, ATTN_HEADS, 2, ATTN_HEAD_DIM)
    k = k.reshape(b, l, ATTN_HEADS, 2, ATTN_HEAD_DIM)
    v = v.reshape(b, l, ATTN_HEADS, ATTN_V_DIM)
    lam_init = 0.8 - 0.6 * math.exp(-0.3 * layer_idx)
    lqk = jnp.sum(lambda_q.astype(f32) * lambda_k.astype(f32), axis=-1)
    lam = jnp.exp(lqk[0]) - jnp.exp(lqk[1]) + lam_init
    attn = attend_fn(q, k, v, lam)
    attn = (rms_norm(attn, subln_g) * (1.0 - lam_init)).reshape(b, l, ATTN_WIDTH)
    conv_in = jnp.concatenate([conv_buf.astype(xbc.dtype), xbc], axis=1)
    acc = conv_b
    for j in range(SSM_CONV):
        acc = acc + conv_w[j] * conv_in[:, j:j + l]
    xbc_c = jax.nn.silu(acc)
    new_conv = conv_in[:, conv_in.shape[1] - (SSM_CONV - 1):]
    xs, bm, cm = jnp.split(xbc_c, [SSM_WIDTH, SSM_WIDTH + SSM_GROUPS * SSM_STATE], axis=-1)
    dt = jax.nn.softplus((dt_raw + dt_bias).astype(f32))
    a = -jnp.exp(a_log.astype(f32))
    xs = xs.reshape(b, l, SSM_HEADS, SSM_HEAD_DIM).astype(f32)
    y, h_new = ssd_scan(xs * dt[..., None], dt * a,
                        bm.reshape(b, l, SSM_GROUPS, SSM_STATE).astype(f32),
                        cm.reshape(b, l, SSM_GROUPS, SSM_STATE).astype(f32), h0)
    y = y + d_skip.astype(f32)[:, None] * xs
    y = y.reshape(b, l, SSM_WIDTH) * jax.nn.silu(z.astype(f32))
    y = rms_norm(y.reshape(b, l, SSM_GROUPS, SSM_WIDTH // SSM_GROUPS),
                 ssm_norm_g.reshape(SSM_GROUPS, -1)).reshape(b, l, SSM_WIDTH)
    h = x + jnp.concatenate([attn, y.astype(x.dtype)], axis=-1) @ w_out
    out = h + peer_ffn(rms_norm(h, g_ffn), peer_wq, peer_keys, peer_u, peer_v)
    return out, k, v, new_conv, h_new.astype(h0.dtype)


def setup_inputs(seed: int = 0) -> dict:
    key = jax.random.key(seed)
    ks = jax.random.split(key, 26)
    n_pages = PAST_LEN // PAGE_SIZE
    n_used = DEC_BATCH * n_pages
    n_pool = n_used + max(1, n_used // 4)

    def nrm(k, shape, scale):
        return scale * jax.random.normal(k, shape, jnp.float32)

    dt0 = jnp.exp(jax.random.uniform(ks[14], (DEPTH, SSM_HEADS), jnp.float32,
                                     math.log(1e-3), math.log(1e-1)))
    return {
        'x_prompt': nrm(ks[0], (BATCH, SEQ, D_MODEL), 1.0),
        'x_sample': nrm(ks[1], (DEC_BATCH, DEC_SEQ, D_MODEL), 1.0),
        'cache_k': nrm(ks[2], (DEPTH, n_pool, PAGE_SIZE, ATTN_HEADS, 2, ATTN_HEAD_DIM), 1.0),
        'cache_v': nrm(ks[3], (DEPTH, n_pool, PAGE_SIZE, ATTN_HEADS, ATTN_V_DIM), 1.0),
        'state_conv': nrm(ks[4], (DEPTH, DEC_BATCH, SSM_CONV - 1, SSM_CONV_DIM), 1.0),
        'state_ssm': nrm(ks[5], (DEPTH, DEC_BATCH, SSM_HEADS, SSM_HEAD_DIM, SSM_STATE), 0.1),
        'page_table': jax.random.permutation(ks[6], n_pool)[:n_used].reshape(DEC_BATCH, n_pages).astype(jnp.int32),
        'g_mix': 1.0 + nrm(ks[7], (DEPTH, D_MODEL), 0.01),
        'w_in': nrm(ks[8], (DEPTH, D_MODEL, IN_PROJ_DIM), D_MODEL ** -0.5),
        'lambda_q': nrm(ks[9], (DEPTH, 2, ATTN_HEAD_DIM), 0.1),
        'lambda_k': nrm(ks[10], (DEPTH, 2, ATTN_HEAD_DIM), 0.1),
        'subln_g': 1.0 + nrm(ks[11], (DEPTH, ATTN_V_DIM), 0.01),
        'conv_w': nrm(ks[12], (DEPTH, SSM_CONV, SSM_CONV_DIM), SSM_CONV ** -0.5),
        'conv_b': nrm(ks[13], (DEPTH, SSM_CONV_DIM), 0.01),
        'dt_bias': dt0 + jnp.log(-jnp.expm1(-dt0)),
        'a_log': jnp.log(jax.random.uniform(ks[15], (DEPTH, SSM_HEADS), jnp.float32, 1.0, 16.0)),
        'd_skip': 1.0 + nrm(ks[16], (DEPTH, SSM_HEADS), 0.01),
        'ssm_norm_g': 1.0 + nrm(ks[17], (DEPTH, SSM_WIDTH), 0.01),
        'w_out': nrm(ks[18], (DEPTH, D_MODEL, D_MODEL), D_MODEL ** -0.5),
        'g_ffn': 1.0 + nrm(ks[19], (DEPTH, D_MODEL), 0.01),
        'peer_wq': nrm(ks[20], (DEPTH, D_MODEL, PEER_HEADS * PEER_KEY_DIM), D_MODEL ** -0.5),
        'peer_keys': nrm(ks[21], (DEPTH, PEER_HEADS, 2, PEER_KEYS, PEER_HALF), PEER_HALF ** -0.5),
        'peer_u': nrm(ks[22], (DEPTH, PEER_EXPERTS, D_MODEL), D_MODEL ** -0.5),
        'peer_v': nrm(ks[23], (DEPTH, PEER_EXPERTS, D_MODEL), PEER_HEADS ** -0.5),
        'g_final': 1.0 + nrm(ks[24], (D_MODEL,), 0.01),
    }


def reference(x_prompt, x_sample, cache_k, cache_v, state_conv, state_ssm, page_table,
              g_mix, w_in, lambda_q, lambda_k, subln_g, conv_w, conv_b, dt_bias, a_log,
              d_skip, ssm_norm_g, w_out, g_ffn, peer_wq, peer_keys, peer_u, peer_v, g_final):
    yp, ys = x_prompt, x_sample
    kp, vp, cp, sp, ksm, vsm, csm, ssm = [], [], [], [], [], [], [], []
    for i in range(DEPTH):
        lw = (g_mix[i], w_in[i], lambda_q[i], lambda_k[i], subln_g[i], conv_w[i], conv_b[i],
              dt_bias[i], a_log[i], d_skip[i], ssm_norm_g[i], w_out[i], g_ffn[i],
              peer_wq[i], peer_keys[i], peer_u[i], peer_v[i])
        conv0 = jnp.zeros((yp.shape[0], SSM_CONV - 1, SSM_CONV_DIM), yp.dtype)
        h0 = jnp.zeros((yp.shape[0], SSM_HEADS, SSM_HEAD_DIM, SSM_STATE), yp.dtype)
        yp, k_p, v_p, c_p, s_p = hybrid_layer(yp, conv0, h0, prompt_attention, i, *lw)
        ck, cv = cache_k[i], cache_v[i]

        def sample_attend(q, k, v, lam, ck=ck, cv=cv):
            return sample_attention(q, k, v, ck, cv, page_table, lam)

        ys, k_s, v_s, c_s, s_s = hybrid_layer(ys, state_conv[i], state_ssm[i], sample_attend, i, *lw)
        kp.append(k_p); vp.append(v_p); cp.append(c_p); sp.append(s_p)
        ksm.append(k_s); vsm.append(v_s); csm.append(c_s); ssm.append(s_s)
    y_prompt = rms_norm(yp, g_final)
    y_sample = rms_norm(ys, g_final)
    return (y_prompt, y_sample, jnp.stack(kp), jnp.stack(vp), jnp.stack(cp), jnp.stack(sp),
            jnp.stack(ksm), jnp.stack(vsm), jnp.stack(csm), jnp.stack(ssm))
```

```python
import functools
import math

import jax
import jax.numpy as jnp
import numpy as np
from jax import lax
from jax.experimental import pallas as pl
from jax.experimental.pallas import tpu as pltpu

F32 = jnp.float32
BF16 = jnp.bfloat16

ATTN_HEADS = 8
ATTN_HEAD_DIM = 128
ATTN_V_DIM = 256
ATTN_WIDTH = ATTN_HEADS * ATTN_V_DIM
SSM_WIDTH = 2048
SSM_HEAD_DIM = 64
SSM_HEADS = 32
SSM_GROUPS = 8
SSM_STATE = 128
SSM_CONV = 4
SSM_CHUNK = 128
SSM_CONV_DIM = SSM_WIDTH + 2 * SSM_GROUPS * SSM_STATE
GROUP_WIDTH = SSM_WIDTH // SSM_GROUPS
PEER_HEADS = 8
PEER_KEYS = 128
PEER_HALF = 128
PEER_TOPK = 16
NORM_EPS = 1e-6
LAM_INIT = 0.8 - 0.6 * math.exp(-0.3 * 0)
ATTN_SCALE = ATTN_HEAD_DIM ** -0.5
LANES = 128
VMEM_LIMIT = 56 * 1024 * 1024


def _params(sem):
    return pltpu.CompilerParams(dimension_semantics=sem, vmem_limit_bytes=VMEM_LIMIT)


def _split3(x):
    x1 = x.astype(BF16)
    r1 = x - x1.astype(F32)
    x2 = r1.astype(BF16)
    x3 = (r1 - x2.astype(F32)).astype(BF16)
    return x1, x2, x3


def _dot(a, b):
    return jnp.dot(a, b, preferred_element_type=F32)


def _dot_nt(a, b):
    return lax.dot_general(a, b, (((1,), (1,)), ((), ())), preferred_element_type=F32)


def _dot_exact_rhs(x, w16):
    x1, x2, x3 = _split3(x)
    return _dot(x1, w16) + _dot(x2, w16) + _dot(x3, w16)


def _dot_exact_lhs(w16, x):
    x1, x2, x3 = _split3(x)
    return _dot(w16, x1) + _dot(w16, x2) + _dot(w16, x3)


def _silu(x):
    return x / (1.0 + jnp.exp(-x))


def _softplus(x):
    return jnp.maximum(x, 0.0) + jnp.log1p(jnp.exp(-jnp.abs(x)))


def _lambda(lq_ref, lk_ref):
    e = jnp.exp(jnp.sum(lq_ref[...] * lk_ref[...], axis=-1, keepdims=True))
    return e[0:1, :] - e[1:2, :] + LAM_INIT


def _norm_mm_kernel(x_ref, g_ref, w_ref, *rest, segs, emit_xn):
    outs = list(rest[:-1])
    xn_scr = rest[-1]
    j = pl.program_id(1)

    @pl.when(j == 0)
    def _():
        x = x_ref[...]
        ms = jnp.mean(x * x, axis=-1, keepdims=True)
        xn = (x * lax.rsqrt(ms + NORM_EPS) * g_ref[...]).astype(BF16)
        xn_scr[...] = xn
        if emit_xn:
            outs[-1][...] = xn

    acc = _dot(xn_scr[...], w_ref[...])
    k = 0
    for (start, nblk, kind) in segs:
        refs = []
        if kind in ("f32", "both"):
            refs.append(outs[k]); k += 1
        if kind in ("bf16", "both"):
            refs.append(outs[k]); k += 1

        @pl.when((j >= start) & (j < start + nblk))
        def _(refs=refs):
            for r in refs:
                r[...] = acc.astype(r.dtype)


def _norm_mm(x, g, w16, segs, *, tm, tn, emit_xn):
    m, kdim = x.shape
    nblocks = sum(wd for _, wd, _ in segs) // tn
    bsegs = [(s // tn, wd // tn, kind) for s, wd, kind in segs]
    out_shapes, out_specs = [], []
    for (sb, nb, kind), (_, wd, _) in zip(bsegs, segs):
        def imap(i, j, sb=sb, nb=nb):
            return (i, jnp.clip(j - sb, 0, nb - 1))
        for dt, want in ((F32, kind in ("f32", "both")), (BF16, kind in ("bf16", "both"))):
            if want:
                out_shapes.append(jax.ShapeDtypeStruct((m, wd), dt))
                out_specs.append(pl.BlockSpec((tm, tn), imap))
    if emit_xn:
        out_shapes.append(jax.ShapeDtypeStruct((m, kdim), BF16))
        out_specs.append(pl.BlockSpec((tm, kdim), lambda i, j: (i, 0)))
    return pl.pallas_call(
        functools.partial(_norm_mm_kernel, segs=bsegs, emit_xn=emit_xn),
        grid=(m // tm, nblocks),
        in_specs=[pl.BlockSpec((tm, kdim), lambda i, j: (i, 0)),
                  pl.BlockSpec((1, kdim), lambda i, j: (0, 0)),
                  pl.BlockSpec((kdim, tn), lambda i, j: (0, j))],
        out_specs=out_specs,
        out_shape=out_shapes,
        scratch_shapes=[pltpu.VMEM((tm, kdim), BF16)],
        compiler_params=_params(("parallel", "arbitrary")),
        name="norm_mm",
    )(x, g, w16)


def _mm_kernel(*refs, n_pairs, has_res):
    o_ref = refs[-1]
    acc = _dot(refs[0][...], refs[n_pairs][...])
    for i in range(1, n_pairs):
        acc = acc + _dot(refs[i][...], refs[n_pairs + i][...])
    if has_res:
        acc = acc + refs[2 * n_pairs][...]
    o_ref[...] = acc


def _mm(a_list, w_list, res, *, tm, tn):
    m = a_list[0].shape[0]
    n = w_list[0].shape[1]
    in_specs = [pl.BlockSpec((tm, a.shape[1]), lambda i, j: (i, 0)) for a in a_list]
    in_specs += [pl.BlockSpec((w.shape[0], tn), lambda i, j: (0, j)) for w in w_list]
    args = list(a_list) + list(w_list)
    if res is not None:
        in_specs.append(pl.BlockSpec((tm, tn), lambda i, j: (i, j)))
        args.append(res)
    return pl.pallas_call(
        functools.partial(_mm_kernel, n_pairs=len(a_list), has_res=res is not None),
        grid=(m // tm, n // tn),
        in_specs=in_specs,
        out_specs=pl.BlockSpec((tm, tn), lambda i, j: (i, j)),
        out_shape=jax.ShapeDtypeStruct((m, n), F32),
        compiler_params=_params(("parallel", "parallel")),
        name="mm",
    )(*args)


def _attn_kernel(slopes_ref, q_ref, k_ref, v_ref, lq_ref, lk_ref, g_ref, o_ref,
                 m_scr, l_scr, acc_scr, *, tq, tk, nk):
    h = pl.program_id(0)
    qi = pl.program_id(1)
    kj = pl.program_id(2)

    @pl.when(kj == 0)
    def _():
        m_scr[...] = jnp.full(m_scr.shape, -jnp.inf, F32)
        l_scr[...] = jnp.zeros(l_scr.shape, F32)
        acc_scr[...] = jnp.zeros(acc_scr.shape, F32)

    @pl.when(kj * tk <= qi * tq + (tq - 1))
    def _():
        slope = slopes_ref[h]
        q = q_ref[...]
        k = k_ref[...]
        v = v_ref[...]
        row = qi * tq + lax.broadcasted_iota(jnp.int32, (tq, tk), 0)
        col = kj * tk + lax.broadcasted_iota(jnp.int32, (tq, tk), 1)
        dist = row - col
        bias = slope * dist.astype(F32)
        for j in range(2):
            s = _dot_nt(q[:, j * 128:(j + 1) * 128], k[:, j * 128:(j + 1) * 128])
            s = s * ATTN_SCALE - bias
            s = jnp.where(dist >= 0, s, -jnp.inf)
            m_prev = m_scr[j]
            m_new = jnp.maximum(m_prev, jnp.max(s, axis=-1, keepdims=True))
            alpha = jnp.exp(m_prev - m_new)
            p = jnp.exp(s - m_new[:, :1])
            l_scr[j] = alpha * l_scr[j] + jnp.sum(p, axis=-1, keepdims=True)
            acc_scr[j] = alpha[:, :1] * acc_scr[j] + _dot(p.astype(BF16), v)
            m_scr[j] = m_new

    @pl.when(kj == nk - 1)
    def _():
        lam = _lambda(lq_ref, lk_ref)
        o0 = acc_scr[0] / l_scr[0][:, :1]
        o1 = acc_scr[1] / l_scr[1][:, :1]
        d = o0 - lam * o1
        ms = jnp.mean(d * d, axis=-1, keepdims=True)
        y = d * lax.rsqrt(ms + NORM_EPS) * g_ref[...]
        o_ref[...] = (y * (1.0 - LAM_INIT)).astype(o_ref.dtype)


def _prompt_attention(q16, k16, v16, lam_q, lam_k, subln_g, *, tq, tk):
    l = q16.shape[0]
    nq, nk = l // tq, l // tk
    slopes = jnp.asarray(2.0 ** (-8.0 * np.arange(1, ATTN_HEADS + 1) / ATTN_HEADS), F32)

    def kv_map(h, qi, kj):
        return (jnp.minimum(kj, (qi * tq + tq - 1) // tk), h)

    return pl.pallas_call(
        functools.partial(_attn_kernel, tq=tq, tk=tk, nk=nk),
        grid=(ATTN_HEADS, nq, nk),
        in_specs=[pl.BlockSpec(memory_space=pltpu.SMEM),
                  pl.BlockSpec((tq, 256), lambda h, qi, kj: (qi, h)),
                  pl.BlockSpec((tk, 256), kv_map),
                  pl.BlockSpec((tk, 256), kv_map),
                  pl.BlockSpec((2, 128), lambda h, qi, kj: (0, 0)),
                  pl.BlockSpec((2, 128), lambda h, qi, kj: (0, 0)),
                  pl.BlockSpec((1, 256), lambda h, qi, kj: (0, 0))],
        out_specs=pl.BlockSpec((tq, 256), lambda h, qi, kj: (qi, h)),
        out_shape=jax.ShapeDtypeStruct((l, ATTN_WIDTH), BF16),
        scratch_shapes=[pltpu.VMEM((2, tq, LANES), F32),
                        pltpu.VMEM((2, tq, LANES), F32),
                        pltpu.VMEM((2, tq, ATTN_V_DIM), F32)],
        compiler_params=_params(("parallel", "parallel", "arbitrary")),
        name="prompt_attn",
    )(slopes, q16, k16, v16, lam_q, lam_k, subln_g)


def _sample_attn_kernel(pt_ref, q_ref, kn_ref, vn_ref, ck_ref, cv_ref, slope_ref, lq_ref, lk_ref,
                        g_ref, o_ref, qbd_scr, m_scr, l_scr, acc_scr, *, n_pages, page):
    del pt_ref
    p = pl.program_id(1)
    width = 2 * ATTN_HEADS * ATTN_HEAD_DIM
    past = n_pages * page

    @pl.when(p == 0)
    def _():
        r = lax.broadcasted_iota(jnp.int32, (16, width), 0)
        lane = lax.broadcasted_iota(jnp.int32, (16, width), 1)
        own = lane // 128 == (r % 8) * 2 + r // 8
        qbd_scr[...] = jnp.where(own, jnp.broadcast_to(q_ref[0], (16, width)), 0.0).astype(BF16)
        m_scr[...] = jnp.full(m_scr.shape, -jnp.inf, F32)
        l_scr[...] = jnp.zeros(l_scr.shape, F32)
        acc_scr[...] = jnp.zeros(acc_scr.shape, F32)

    slope = slope_ref[...]
    kpos = p * page + lax.broadcasted_iota(jnp.int32, (16, page), 1)
    s = _dot_nt(qbd_scr[...], ck_ref[...].astype(BF16))
    s = s * ATTN_SCALE - slope * (past - kpos).astype(F32)
    m_prev = m_scr[...]
    m_new = jnp.maximum(m_prev, jnp.max(s, axis=-1, keepdims=True))
    alpha = jnp.exp(m_prev - m_new)
    pe = jnp.exp(s - m_new)
    l_scr[...] = alpha * l_scr[...] + jnp.sum(pe, axis=-1, keepdims=True)
    acc_scr[...] = alpha[:, :1] * acc_scr[...] + _dot(pe.astype(BF16), cv_ref[...].astype(BF16))
    m_scr[...] = m_new

    @pl.when(p == n_pages - 1)
    def _():
        kn = kn_ref[0].astype(BF16).astype(F32)
        vn = vn_ref[0].astype(BF16).astype(F32)
        s_self = jnp.sum(qbd_scr[...].astype(F32) * kn, axis=-1, keepdims=True) * ATTN_SCALE
        m_prev = m_scr[...]
        m_new = jnp.maximum(m_prev, s_self)
        alpha = jnp.exp(m_prev - m_new)
        p_self = jnp.exp(s_self - m_new)
        l_fin = alpha * l_scr[...] + p_self
        acc = alpha[:, :1] * acc_scr[...] + p_self[:, :1] * vn
        n = acc / l_fin[:, :1]
        lam = _lambda(lq_ref, lk_ref)
        d = n[0:8, :] - lam * n[8:16, :]
        r = lax.broadcasted_iota(jnp.int32, (8, width), 0)
        lane = lax.broadcasted_iota(jnp.int32, (8, width), 1)
        dm = jnp.where(lane // ATTN_V_DIM == r, d, 0.0)
        ss = jnp.sum(dm * dm, axis=-1, keepdims=True)
        dn = dm * lax.rsqrt(ss / ATTN_V_DIM + NORM_EPS)
        o = jnp.sum(dn, axis=0, keepdims=True) * g_ref[...]
        o_ref[0] = o * (1.0 - LAM_INIT)


def _sample_attention(q, k_new, v_new, cache_k, cache_v, page_table, lam_q, lam_k, subln_g):
    b = q.shape[0]
    n_pool, page = cache_k.shape[0], cache_k.shape[1]
    n_pages = page_table.shape[1]
    width = q.shape[1]
    ck = cache_k.reshape(n_pool, page, width)
    cv = cache_v.reshape(n_pool, page, width)
    rows = np.arange(16)
    slopes = 2.0 ** (-8.0 * ((rows % 8) + 1) / ATTN_HEADS)
    slope_tile = jnp.asarray(np.repeat(slopes[:, None], LANES, axis=1), F32)
    g_tiled = jnp.tile(subln_g, (1, ATTN_HEADS))
    row_spec = pl.BlockSpec((1, 1, width), lambda i, p, pt: (i, 0, 0))

    def page_map(i, p, pt):
        return (pt[i * n_pages + p], 0, 0)

    def full(shape):
        return pl.BlockSpec(shape, lambda i, p, pt: tuple(0 for _ in shape))

    out = pl.pallas_call(
        functools.partial(_sample_attn_kernel, n_pages=n_pages, page=page),
        grid_spec=pltpu.PrefetchScalarGridSpec(
            num_scalar_prefetch=1,
            grid=(b, n_pages),
            in_specs=[row_spec, row_spec, row_spec,
                      pl.BlockSpec((None, page, width), page_map),
                      pl.BlockSpec((None, page, width), page_map),
                      full((16, LANES)), full((2, 128)), full((2, 128)), full((1, width))],
            out_specs=pl.BlockSpec((1, 1, width), lambda i, p, pt: (i, 0, 0)),
            scratch_shapes=[pltpu.VMEM((16, width), BF16),
                            pltpu.VMEM((16, LANES), F32),
                            pltpu.VMEM((16, LANES), F32),
                            pltpu.VMEM((16, width), F32)]),
        out_shape=jax.ShapeDtypeStruct((b, 1, width), F32),
        compiler_params=_params(("parallel", "arbitrary")),
        name="sample_attn",
    )(page_table.reshape(-1), q.reshape(b, 1, width), k_new.reshape(b, 1, width),
      v_new.reshape(b, 1, width), ck, cv, slope_tile, lam_q, lam_k, g_tiled)
    return out.reshape(b, width)


def _ssd_prompt_kernel(xbc_ref, z_ref, dtr_ref, cw_ref, cb_ref, dtb_ref, alog_ref, dexp_ref, ng_ref,
                       e_ref, y_ref, hfin_ref, buf, xc, ht, *, n_chunks):
    c = pl.program_id(0)
    q = SSM_CHUNK

    @pl.when(c == 0)
    def _():
        buf[0:8, :] = jnp.zeros((8, SSM_CONV_DIM), F32)
        ht[...] = jnp.zeros(ht.shape, F32)

    buf[8:8 + q, :] = xbc_ref[...]
    blk = 512
    for cbk in range(SSM_CONV_DIM // blk):
        sl = slice(cbk * blk, (cbk + 1) * blk)
        acc = cb_ref[:, sl]
        for j in range(SSM_CONV):
            acc = acc + cw_ref[j:j + 1, sl] * buf[5 + j:5 + j + q, sl]
        xc[:, sl] = _silu(acc)
    buf[0:8, :] = buf[q:q + 8, :]

    dt = _softplus(dtr_ref[...] + dtb_ref[...])
    a = dt * (-jnp.exp(alog_ref[...]))
    ri = lax.broadcasted_iota(jnp.int32, (q, q), 0)
    ci = lax.broadcasted_iota(jnp.int32, (q, q), 1)
    causal = ri >= ci
    tril = jnp.where(causal, 1.0, 0.0).astype(BF16)
    a_cum = _dot_exact_lhs(tril, a)
    a_cum_t = a_cum.T
    e16 = e_ref[...]
    ac_exp = _dot_exact_rhs(a_cum, e16)
    dt_exp = _dot_exact_rhs(dt, e16)
    ac_last = ac_exp[q - 1:q, :]
    xs = xc[:, 0:SSM_WIDTH]
    xdt = xs * dt_exp
    xw = xdt * jnp.exp(ac_last - ac_exp)
    ea = jnp.exp(ac_exp)
    cdec = jnp.exp(ac_last)
    lane_g = lax.broadcasted_iota(jnp.int32, (q, GROUP_WIDTH), 1)

    for g in range(SSM_GROUPS):
        gs = slice(g * GROUP_WIDTH, (g + 1) * GROUP_WIDTH)
        bm = xc[:, SSM_WIDTH + g * SSM_STATE:SSM_WIDTH + (g + 1) * SSM_STATE]
        cm = xc[:, SSM_WIDTH + (SSM_GROUPS + g) * SSM_STATE:SSM_WIDTH + (SSM_GROUPS + g + 1) * SSM_STATE]
        bm16 = bm.astype(BF16)
        cm16 = cm.astype(BF16)
        cbm = _dot_nt(cm16, bm16)
        xdt16 = xdt[:, gs].astype(BF16)
        yg = jnp.zeros((q, GROUP_WIDTH), F32)
        for r in range(SSM_HEADS // SSM_GROUPS):
            hd = g * (SSM_HEADS // SSM_GROUPS) + r
            seg = a_cum[:, hd:hd + 1] - a_cum_t[hd:hd + 1, :]
            dec = jnp.exp(jnp.where(causal, seg, -jnp.inf))
            yr = _dot((cbm * dec).astype(BF16), xdt16)
            yg = jnp.where(lane_g // SSM_HEAD_DIM == r, yr, yg)
        h_in = ht[:, gs]
        y_off = _dot(cm16, h_in.astype(BF16)) * ea[:, gs]
        st = _dot(bm.T.astype(BF16), xw[:, gs].astype(BF16))
        ht[:, gs] = cdec[:, gs] * h_in + st
        y = yg + y_off + dexp_ref[:, gs] * xs[:, gs]
        gated = y * _silu(z_ref[:, gs])
        ms = jnp.mean(gated * gated, axis=-1, keepdims=True)
        y_ref[:, gs] = (gated * lax.rsqrt(ms + NORM_EPS) * ng_ref[:, gs]).astype(y_ref.dtype)

    @pl.when(c == n_chunks - 1)
    def _():
        hfin_ref[...] = ht[...].T


def _head_expand_matrix():
    e = np.zeros((LANES, SSM_WIDTH), np.float32)
    for hd in range(SSM_HEADS):
        e[hd, hd * SSM_HEAD_DIM:(hd + 1) * SSM_HEAD_DIM] = 1.0
    return jnp.asarray(e, BF16)


def _pad_lanes(v):
    return jnp.pad(v.reshape(1, -1), ((0, 0), (0, LANES - v.size)))


def _ssd_prompt(xbc, z, dt_raw, conv_w, conv_b, dt_bias, a_log, d_skip, norm_g):
    l = xbc.shape[0]
    n_chunks = l // SSM_CHUNK
    q = SSM_CHUNK

    def full(shape):
        return pl.BlockSpec(shape, lambda c: tuple(0 for _ in shape))

    y16, hfin = pl.pallas_call(
        functools.partial(_ssd_prompt_kernel, n_chunks=n_chunks),
        grid=(n_chunks,),
        in_specs=[pl.BlockSpec((q, SSM_CONV_DIM), lambda c: (c, 0)),
                  pl.BlockSpec((q, SSM_WIDTH), lambda c: (c, 0)),
                  pl.BlockSpec((q, LANES), lambda c: (c, 0)),
                  full((SSM_CONV, SSM_CONV_DIM)), full((1, SSM_CONV_DIM)),
                  full((1, LANES)), full((1, LANES)), full((1, SSM_WIDTH)), full((1, SSM_WIDTH)),
                  full((LANES, SSM_WIDTH))],
        out_specs=[pl.BlockSpec((q, SSM_WIDTH), lambda c: (c, 0)),
                   full((SSM_WIDTH, SSM_STATE))],
        out_shape=[jax.ShapeDtypeStruct((l, SSM_WIDTH), BF16),
                   jax.ShapeDtypeStruct((SSM_WIDTH, SSM_STATE), F32)],
        scratch_shapes=[pltpu.VMEM((q + 8, SSM_CONV_DIM), F32),
                        pltpu.VMEM((q, SSM_CONV_DIM), F32),
                        pltpu.VMEM((SSM_STATE, SSM_WIDTH), F32)],
        compiler_params=_params(("arbitrary",)),
        name="ssd_prompt",
    )(xbc, z, dt_raw, conv_w, conv_b.reshape(1, -1), _pad_lanes(dt_bias), _pad_lanes(a_log),
      jnp.repeat(d_skip, SSM_HEAD_DIM).reshape(1, -1), norm_g.reshape(1, -1), _head_expand_matrix())
    return y16, hfin


def _ssd_sample_pre_kernel(xbc_ref, dtr_ref, sc_ref, cw_ref, cb_ref, dtb_ref, aexp_ref, e_ref,
                           xc_ref, cn_ref, xdt_ref, dec_ref):
    xbc = xbc_ref[...]
    acc = cb_ref[...] + cw_ref[SSM_CONV - 1:SSM_CONV, :] * xbc
    for j in range(SSM_CONV - 1):
        acc = acc + cw_ref[j:j + 1, :] * sc_ref[j]
    xcv = _silu(acc)
    xc_ref[...] = xcv
    cn_ref[0] = sc_ref[1]
    cn_ref[1] = sc_ref[2]
    cn_ref[2] = xbc
    dt = _softplus(dtr_ref[...] + dtb_ref[...])
    dt_exp = _dot_exact_rhs(dt, e_ref[...])
    dec_ref[...] = jnp.exp(dt_exp * (-jnp.exp(aexp_ref[...])))
    xdt_ref[...] = xcv[:, 0:SSM_WIDTH] * dt_exp


def _ssd_sample_state_kernel(xdt_ref, dec_ref, bm_ref, cm_ref, s_ref, so_ref, y_ref):
    nb = xdt_ref.shape[0]
    xdt_k = xdt_ref[...]
    dec_k = dec_ref[...]
    bm = bm_ref[...]
    cm = cm_ref[...]
    lane = lax.broadcasted_iota(jnp.int32, (nb, LANES), 1)
    yk = jnp.zeros((nb, LANES), F32)
    for j in range(LANES):
        h0 = s_ref[:, j * SSM_STATE:(j + 1) * SSM_STATE]
        hn = dec_k[:, j:j + 1] * h0 + xdt_k[:, j:j + 1] * bm
        so_ref[:, j * SSM_STATE:(j + 1) * SSM_STATE] = hn
        yk = jnp.where(lane == j, jnp.sum(hn * cm, axis=-1, keepdims=True), yk)
    y_ref[...] = yk


def _ssd_sample_post_kernel(y_ref, xc_ref, z_ref, dexp_ref, ng_ref, o_ref):
    for g in range(SSM_GROUPS):
        gs = slice(g * GROUP_WIDTH, (g + 1) * GROUP_WIDTH)
        y = y_ref[:, gs] + dexp_ref[:, gs] * xc_ref[:, gs]
        gated = y * _silu(z_ref[:, gs])
        ms = jnp.mean(gated * gated, axis=-1, keepdims=True)
        o_ref[:, gs] = gated * lax.rsqrt(ms + NORM_EPS) * ng_ref[:, gs]


def _ssd_sample(xbc, z, dt_raw, state_conv, state_ssm, conv_w, conv_b, dt_bias, a_log, d_skip, norm_g):
    nb = xbc.shape[0]
    n_pairs = SSM_HEADS // 2
    pair_w = 2 * SSM_HEAD_DIM * SSM_STATE
    sc = jnp.swapaxes(state_conv, 0, 1)
    st = state_ssm.reshape(nb, SSM_HEADS * SSM_HEAD_DIM * SSM_STATE)
    xc, cn, xdt, dec = pl.pallas_call(
        _ssd_sample_pre_kernel,
        out_shape=[jax.ShapeDtypeStruct((nb, SSM_CONV_DIM), F32),
                   jax.ShapeDtypeStruct((SSM_CONV - 1, nb, SSM_CONV_DIM), F32),
                   jax.ShapeDtypeStruct((nb, SSM_WIDTH), F32),
                   jax.ShapeDtypeStruct((nb, SSM_WIDTH), F32)],
        compiler_params=pltpu.CompilerParams(vmem_limit_bytes=VMEM_LIMIT),
        name="ssd_sample_pre",
    )(xbc, dt_raw, sc, conv_w, conv_b.reshape(1, -1), _pad_lanes(dt_bias),
      jnp.repeat(a_log, SSM_HEAD_DIM).reshape(1, -1), _head_expand_matrix())
    b_blk0 = SSM_WIDTH // LANES
    c_blk0 = b_blk0 + SSM_GROUPS
    so, yraw = pl.pallas_call(
        _ssd_sample_state_kernel,
        grid=(n_pairs,),
        in_specs=[pl.BlockSpec((nb, LANES), lambda k: (0, k)),
                  pl.BlockSpec((nb, LANES), lambda k: (0, k)),
                  pl.BlockSpec((nb, LANES), lambda k: (0, b_blk0 + k // 2)),
                  pl.BlockSpec((nb, LANES), lambda k: (0, c_blk0 + k // 2)),
                  pl.BlockSpec((nb, pair_w), lambda k: (0, k))],
        out_specs=[pl.BlockSpec((nb, pair_w), lambda k: (0, k)),
                   pl.BlockSpec((nb, LANES), lambda k: (0, k))],
        out_shape=[jax.ShapeDtypeStruct(st.shape, F32),
                   jax.ShapeDtypeStruct((nb, SSM_WIDTH), F32)],
        compiler_params=_params(("parallel",)),
        name="ssd_sample_state",
    )(xdt, dec, xc, xc, st)
    y = pl.pallas_call(
        _ssd_sample_post_kernel,
        out_shape=jax.ShapeDtypeStruct((nb, SSM_WIDTH), F32),
        compiler_params=pltpu.CompilerParams(vmem_limit_bytes=VMEM_LIMIT),
        name="ssd_sample_post",
    )(yraw, xc, z, jnp.repeat(d_skip, SSM_HEAD_DIM).reshape(1, -1), norm_g.reshape(1, -1))
    return y, jnp.swapaxes(cn, 0, 1), so.reshape(state_ssm.shape)


def _peer_query_kernel(h_ref, g_ref, wqt_ref, xnt_ref, qt_ref, xnt_scr):
    j = pl.program_id(1)

    @pl.when(j == 0)
    def _():
        x = h_ref[...]
        ms = jnp.mean(x * x, axis=-1, keepdims=True)
        xn = x * lax.rsqrt(ms + NORM_EPS) * g_ref[...]
        xnt = xn.T.astype(BF16)
        xnt_scr[...] = xnt
        xnt_ref[...] = xnt

    qt_ref[...] = _dot(wqt_ref[...], xnt_scr[...])


def _peer_query(h, g, wqt16, *, tm, tr):
    m, d = h.shape
    nq = wqt16.shape[0]
    return pl.pallas_call(
        _peer_query_kernel,
        grid=(m // tm, nq // tr),
        in_specs=[pl.BlockSpec((tm, d), lambda i, j: (i, 0)),
                  pl.BlockSpec((1, d), lambda i, j: (0, 0)),
                  pl.BlockSpec((tr, d), lambda i, j: (j, 0))],
        out_specs=[pl.BlockSpec((d, tm), lambda i, j: (0, i)),
                   pl.BlockSpec((tr, tm), lambda i, j: (j, i))],
        out_shape=[jax.ShapeDtypeStruct((d, m), BF16),
                   jax.ShapeDtypeStruct((nq, m), F32)],
        scratch_shapes=[pltpu.VMEM((d, tm), BF16)],
        compiler_params=_params(("parallel", "arbitrary")),
        name="peer_query",
    )(h, g, wqt16)


def _peer_retrieve_kernel(qt_ref, keys_ref, s_ref, r_ref, pr_ref):
    tb = qt_ref.shape[1]
    nkeys = PEER_KEYS
    iota_n = lax.broadcasted_iota(jnp.int32, (nkeys, tb), 0)
    npair = PEER_TOPK * PEER_TOPK
    iota_p = lax.broadcasted_iota(jnp.int32, (npair, tb), 0)
    for h in range(PEER_HEADS):
        tops = []
        for j in range(2):
            hj = 2 * h + j
            s = _dot(keys_ref[hj], qt_ref[hj * PEER_HALF:(hj + 1) * PEER_HALF, :].astype(BF16))
            s_ref[hj] = s
            cur = s
            rank = jnp.full((nkeys, tb), float(PEER_TOPK), F32)
            vals = []
            for k in range(PEER_TOPK):
                m = jnp.max(cur, axis=0, keepdims=True)
                first = jnp.min(jnp.where(cur == m, iota_n, nkeys), axis=0, keepdims=True)
                hit = iota_n == first
                rank = jnp.where(hit, float(k), rank)
                cur = jnp.where(hit, -jnp.inf, cur)
                vals.append(m)
            r_ref[hj] = rank
            tops.append(vals)
        v2 = jnp.concatenate(tops[1], axis=0)
        cand = jnp.concatenate([tops[0][ka] + v2 for ka in range(PEER_TOPK)], axis=0)
        picked = []
        pos = None
        for k in range(PEER_TOPK):
            m = jnp.max(cand, axis=0, keepdims=True)
            pos = jnp.min(jnp.where(cand == m, iota_p, npair), axis=0, keepdims=True)
            cand = jnp.where(iota_p == pos, -jnp.inf, cand)
            picked.append(m)
        tau = picked[-1]
        tau_up = jnp.full_like(tau, jnp.inf)
        z = jnp.zeros_like(tau)
        for m in picked:
            tau_up = jnp.minimum(tau_up, jnp.where(m > tau, m, jnp.inf))
            z = z + jnp.exp(m - picked[0])
        posf = pos.astype(F32)
        ra = jnp.floor(posf / PEER_TOPK)
        rb = posf - ra * PEER_TOPK
        zero = jnp.zeros_like(tau)
        m1 = tops[0][0]
        m2 = tops[1][0]
        zsum = z * jnp.exp(picked[0] - (m1 + m2))
        pr_ref[h] = jnp.concatenate([tau, tau_up, ra, rb, m1, m2, zsum, zero], axis=0)


def _peer_retrieve(qt, keys16, *, tb):
    m = qt.shape[1]
    nhj = 2 * PEER_HEADS
    return pl.pallas_call(
        _peer_retrieve_kernel,
        grid=(m // tb,),
        in_specs=[pl.BlockSpec((qt.shape[0], tb), lambda i: (0, i)),
                  pl.BlockSpec((nhj, PEER_KEYS, PEER_HALF), lambda i: (0, 0, 0))],
        out_specs=[pl.BlockSpec((nhj, PEER_KEYS, tb), lambda i: (0, 0, i)),
                   pl.BlockSpec((nhj, PEER_KEYS, tb), lambda i: (0, 0, i)),
                   pl.BlockSpec((PEER_HEADS, 8, tb), lambda i: (0, 0, i))],
        out_shape=[jax.ShapeDtypeStruct((nhj, PEER_KEYS, m), F32),
                   jax.ShapeDtypeStruct((nhj, PEER_KEYS, m), F32),
                   jax.ShapeDtypeStruct((PEER_HEADS, 8, m), F32)],
        compiler_params=_params(("parallel",)),
        name="peer_retrieve",
    )(qt, keys16)


def _gelu(x):
    return 0.5 * x * (1.0 + lax.erf(x * (2.0 ** -0.5)))


def _peer_expert_kernel(xnt_ref, s_ref, r_ref, pr_ref, u_ref, vt_ref, o_ref,
                        acc, e2, thr2, c1, thr1, eq1, hd, *, n_eblk, a_per_blk):
    e = pl.program_id(1)
    nk = PEER_KEYS

    @pl.when(e == 0)
    def _():
        acc[...] = jnp.zeros(acc.shape, F32)
        for h in range(PEER_HEADS):
            pr = pr_ref[h]
            tau, tau_up, ra, rb = pr[0:1], pr[1:2], pr[2:3], pr[3:4]
            m1, m2, zsum = pr[4:5], pr[5:6], pr[6:7]
            e2[h] = jnp.exp(s_ref[2 * h + 1] - m2)
            thr2[h] = jnp.where(r_ref[2 * h + 1] <= rb, tau, tau_up)
            c1[h] = jnp.exp(s_ref[2 * h] - m1) / zsum
            thr1[h] = jnp.where(r_ref[2 * h] < ra, tau, tau_up)
            eq1[h] = jnp.where(r_ref[2 * h] == ra, 1.0, 0.0)

    act = _dot(u_ref[...], xnt_ref[...])
    for ai in range(a_per_blk):
        a = e * a_per_blk + ai
        w = jnp.zeros((nk, act.shape[1]), F32)
        for h in range(PEER_HEADS):
            s1 = s_ref[2 * h, pl.ds(a, 1), :]
            thr = jnp.where(eq1[h, pl.ds(a, 1), :] > 0.5, thr2[h], thr1[h, pl.ds(a, 1), :])
            pair = s_ref[2 * h + 1] + s1
            w = w + jnp.where(pair >= thr, e2[h] * c1[h, pl.ds(a, 1), :], 0.0)
        hd[ai * nk:(ai + 1) * nk, :] = (_gelu(act[ai * nk:(ai + 1) * nk, :]) * w).astype(BF16)
    acc[...] += _dot(vt_ref[...], hd[...])

    @pl.when(e == n_eblk - 1)
    def _():
        o_ref[...] = acc[...].T


def _peer_experts(xnt, s, r, pr, u16, vt16, *, tt, eb):
    d, m = xnt.shape
    n_exp = u16.shape[0]
    n_eblk = n_exp // eb
    nhj = 2 * PEER_HEADS
    return pl.pallas_call(
        functools.partial(_peer_expert_kernel, n_eblk=n_eblk, a_per_blk=eb // PEER_KEYS),
        grid=(m // tt, n_eblk),
        in_specs=[pl.BlockSpec((d, tt), lambda i, e: (0, i)),
                  pl.BlockSpec((nhj, PEER_KEYS, tt), lambda i, e: (0, 0, i)),
                  pl.BlockSpec((nhj, PEER_KEYS, tt), lambda i, e: (0, 0, i)),
                  pl.BlockSpec((PEER_HEADS, 8, tt), lambda i, e: (0, 0, i)),
                  pl.BlockSpec((eb, d), lambda i, e: (e, 0)),
                  pl.BlockSpec((d, eb), lambda i, e: (0, e))],
        out_specs=pl.BlockSpec((tt, d), lambda i, e: (i, 0)),
        out_shape=jax.ShapeDtypeStruct((m, d), F32),
        scratch_shapes=[pltpu.VMEM((d, tt), F32)]
        + [pltpu.VMEM((PEER_HEADS, PEER_KEYS, tt), F32) for _ in range(5)]
        + [pltpu.VMEM((eb, tt), BF16)],
        compiler_params=_params(("parallel", "arbitrary")),
        name="peer_experts",
    )(xnt, s, r, pr, u16, vt16)


def _final_kernel(h_ref, f_ref, g_ref, o_ref):
    x = h_ref[...] + f_ref[...]
    ms = jnp.mean(x * x, axis=-1, keepdims=True)
    o_ref[...] = x * lax.rsqrt(ms + NORM_EPS) * g_ref[...]


def _final_norm(h, ffn, g, *, tm):
    m, d = h.shape
    row = pl.BlockSpec((tm, d), lambda i: (i, 0))
    return pl.pallas_call(
        _final_kernel,
        grid=(m // tm,),
        in_specs=[row, row, pl.BlockSpec((1, d), lambda i: (0, 0))],
        out_specs=row,
        out_shape=jax.ShapeDtypeStruct((m, d), F32),
        compiler_params=_params(("parallel",)),
        name="final_norm",
    )(h, ffn, g)


def _tile(m, pref):
    return pref if m % pref == 0 else m


def _in_proj(x, g_mix, w16, wdt16):
    m = x.shape[0]
    tm = _tile(m, 512)
    qw = 2 * ATTN_HEADS * ATTN_HEAD_DIM
    segs = [(0, qw, "bf16"), (qw, qw, "both"), (2 * qw, ATTN_WIDTH, "both"),
            (2 * qw + ATTN_WIDTH, SSM_WIDTH, "f32"), (2 * qw + ATTN_WIDTH + SSM_WIDTH, SSM_CONV_DIM, "f32")]
    q16, k32, k16, v32, v16, z32, xbc32, xn16 = _norm_mm(x, g_mix, w16, segs, tm=tm, tn=512, emit_xn=True)
    dt_raw = _mm([xn16], [wdt16], None, tm=tm, tn=LANES)
    return q16, k32, k16, v32, v16, z32, xbc32, dt_raw


def _peer_and_final(h, g_ffn, wqt16, keys16, u16, vt16, g_final, *, tt):
    m = h.shape[0]
    tm = _tile(m, 512)
    xnt, qt = _peer_query(h, g_ffn, wqt16, tm=tm, tr=512)
    s, r, pr = _peer_retrieve(qt, keys16, tb=LANES)
    ffn = _peer_experts(xnt, s, r, pr, u16, vt16, tt=_tile(m, tt), eb=512)
    return _final_norm(h, ffn, g_final, tm=_tile(m, 256))


def kernel(x_prompt, x_sample, cache_k, cache_v, state_conv, state_ssm, page_table, g_mix, w_in, lambda_q, lambda_k, subln_g, conv_w, conv_b, dt_bias, a_log, d_skip, ssm_norm_g, w_out, g_ffn, peer_wq, peer_keys, peer_u, peer_v, g_final):
    assert w_in.shape[0] == 1, "single trunk layer"
    bsz, seq, dm = x_prompt.shape
    nb = x_sample.shape[0]
    assert bsz == 1 and x_sample.shape[1] == 1
    main_cols = w_in.shape[2] - SSM_HEADS
    w16 = w_in[0].astype(BF16)
    wdt16 = jnp.pad(w_in[0][:, main_cols:], ((0, 0), (0, LANES - SSM_HEADS))).astype(BF16)
    wo16 = w_out[0].astype(BF16)
    wqt16 = peer_wq[0].T.astype(BF16)
    keys16 = peer_keys[0].reshape(2 * PEER_HEADS, PEER_KEYS, PEER_HALF).astype(BF16)
    u16 = peer_u[0].astype(BF16)
    vt16 = peer_v[0].T.astype(BF16)
    gm = g_mix[0].reshape(1, -1)
    gf = g_ffn[0].reshape(1, -1)
    gfin = g_final.reshape(1, -1)
    sg = subln_g[0].reshape(1, -1)
    lq, lk = lambda_q[0], lambda_k[0]

    xp = x_prompt.reshape(seq, dm)
    q16, k32, k16, v32, v16, z32, xbc32, dtr = _in_proj(xp, gm, w16, wdt16)
    attn16 = _prompt_attention(q16, k16, v16, lq, lk, sg, tq=256, tk=512)
    y16, hfin = _ssd_prompt(xbc32, z32, dtr, conv_w[0], conv_b[0], dt_bias[0], a_log[0], d_skip[0],
                            ssm_norm_g[0])
    hp = _mm([attn16, y16], [wo16[:ATTN_WIDTH], wo16[ATTN_WIDTH:]], xp, tm=_tile(seq, 512), tn=512)
    y_prompt = _peer_and_final(hp, gf, wqt16, keys16, u16, vt16, gfin, tt=256)

    xs = x_sample.reshape(nb, dm)
    sq16, sk32, _, sv32, _, sz32, sxbc32, sdtr = _in_proj(xs, gm, w16, wdt16)
    sattn = _sample_attention(sq16.astype(F32), sk32, sv32, cache_k[0], cache_v[0], page_table, lq, lk, sg)
    sy, conv_s, ssm_s = _ssd_sample(sxbc32, sz32, sdtr, state_conv[0], state_ssm[0], conv_w[0], conv_b[0],
                                    dt_bias[0], a_log[0], d_skip[0], ssm_norm_g[0])
    hs = _mm([sattn.astype(BF16), sy.astype(BF16)], [wo16[:ATTN_WIDTH], wo16[ATTN_WIDTH:]], xs,
             tm=nb, tn=512)
    y_sample = _peer_and_final(hs, gf, wqt16, keys16, u16, vt16, gfin, tt=LANES)

    return (y_prompt.reshape(bsz, seq, dm),
            y_sample.reshape(nb, 1, dm),
            k32.reshape(1, bsz, seq, ATTN_HEADS, 2, ATTN_HEAD_DIM),
            v32.reshape(1, bsz, seq, ATTN_HEADS, ATTN_V_DIM),
            xbc32[seq - (SSM_CONV - 1):].reshape(1, bsz, SSM_CONV - 1, SSM_CONV_DIM),
            hfin.reshape(1, bsz, SSM_HEADS, SSM_HEAD_DIM, SSM_STATE),
            sk32.reshape(1, nb, 1, ATTN_HEADS, 2, ATTN_HEAD_DIM),
            sv32.reshape(1, nb, 1, ATTN_HEADS, ATTN_V_DIM),
            conv_s.reshape(1, nb, SSM_CONV - 1, SSM_CONV_DIM),
            ssm_s.reshape(1, nb, SSM_HEADS, SSM_HEAD_DIM, SSM_STATE))
```

```python
import functools
import math

import jax
import jax.numpy as jnp
import numpy as np
from jax import lax
from jax.experimental import pallas as pl
from jax.experimental.pallas import tpu as pltpu

F32 = jnp.float32
BF16 = jnp.bfloat16

ATTN_HEADS = 8
ATTN_HEAD_DIM = 128
ATTN_V_DIM = 256
ATTN_WIDTH = ATTN_HEADS * ATTN_V_DIM
SSM_WIDTH = 2048
SSM_HEAD_DIM = 64
SSM_HEADS = 32
SSM_GROUPS = 8
SSM_STATE = 128
SSM_CONV = 4
SSM_CHUNK = 128
SSM_CONV_DIM = SSM_WIDTH + 2 * SSM_GROUPS * SSM_STATE
GROUP_WIDTH = SSM_WIDTH // SSM_GROUPS
PEER_HEADS = 8
PEER_KEYS = 128
PEER_HALF = 128
PEER_TOPK = 16
NORM_EPS = 1e-6
LAM_INIT = 0.8 - 0.6 * math.exp(-0.3 * 0)
ATTN_SCALE = ATTN_HEAD_DIM ** -0.5
LANES = 128
VMEM_LIMIT = 56 * 1024 * 1024


def _params(sem):
    return pltpu.CompilerParams(dimension_semantics=sem, vmem_limit_bytes=VMEM_LIMIT)


def _split3(x):
    x1 = x.astype(BF16)
    r1 = x - x1.astype(F32)
    x2 = r1.astype(BF16)
    x3 = (r1 - x2.astype(F32)).astype(BF16)
    return x1, x2, x3


def _dot(a, b):
    return jnp.dot(a, b, preferred_element_type=F32)


def _dot_nt(a, b):
    return lax.dot_general(a, b, (((1,), (1,)), ((), ())), preferred_element_type=F32)


def _dot_exact_rhs(x, w16):
    x1, x2, x3 = _split3(x)
    return _dot(x1, w16) + _dot(x2, w16) + _dot(x3, w16)


def _dot_exact_lhs(w16, x):
    x1, x2, x3 = _split3(x)
    return _dot(w16, x1) + _dot(w16, x2) + _dot(w16, x3)


def _silu(x):
    return x / (1.0 + jnp.exp(-x))


def _softplus(x):
    return jnp.maximum(x, 0.0) + jnp.log1p(jnp.exp(-jnp.abs(x)))


def _lambda(lq_ref, lk_ref):
    e = jnp.exp(jnp.sum(lq_ref[...] * lk_ref[...], axis=-1, keepdims=True))
    return e[0:1, :] - e[1:2, :] + LAM_INIT


def _alibi_slopes():
    return 2.0 ** (-8.0 * np.arange(1, ATTN_HEADS + 1) / ATTN_HEADS)


def _norm_mm_kernel(x_ref, g_ref, w_ref, *rest, segs, emit_xn):
    outs = list(rest[:-1])
    xn_scr = rest[-1]
    j = pl.program_id(1)

    @pl.when(j == 0)
    def _():
        x = x_ref[...]
        ms = jnp.mean(x * x, axis=-1, keepdims=True)
        xn = (x * lax.rsqrt(ms + NORM_EPS) * g_ref[...]).astype(BF16)
        xn_scr[...] = xn
        if emit_xn:
            outs[-1][...] = xn

    acc = _dot(xn_scr[...], w_ref[...])
    k = 0
    for (start, nblk, kind, scale) in segs:
        refs = []
        if kind in ("f32", "both"):
            refs.append(outs[k]); k += 1
        if kind in ("bf16", "both"):
            refs.append(outs[k]); k += 1

        @pl.when((j >= start) & (j < start + nblk))
        def _(refs=refs, scale=scale):
            val = acc if scale == 1.0 else acc * scale
            for r in refs:
                r[...] = val.astype(r.dtype)


def _norm_mm(x, g, w16, segs, *, tm, tn, emit_xn):
    m, kdim = x.shape
    nblocks = sum(wd for _, wd, _, _ in segs) // tn
    bsegs = [(s // tn, wd // tn, kind, scale) for s, wd, kind, scale in segs]
    out_shapes, out_specs = [], []
    for (sb, nb, kind, _), (_, wd, _, _) in zip(bsegs, segs):
        def imap(i, j, sb=sb, nb=nb):
            return (i, jnp.clip(j - sb, 0, nb - 1))
        for dt, want in ((F32, kind in ("f32", "both")), (BF16, kind in ("bf16", "both"))):
            if want:
                out_shapes.append(jax.ShapeDtypeStruct((m, wd), dt))
                out_specs.append(pl.BlockSpec((tm, tn), imap))
    if emit_xn:
        out_shapes.append(jax.ShapeDtypeStruct((m, kdim), BF16))
        out_specs.append(pl.BlockSpec((tm, kdim), lambda i, j: (i, 0)))
    return pl.pallas_call(
        functools.partial(_norm_mm_kernel, segs=bsegs, emit_xn=emit_xn),
        grid=(m // tm, nblocks),
        in_specs=[pl.BlockSpec((tm, kdim), lambda i, j: (i, 0)),
                  pl.BlockSpec((1, kdim), lambda i, j: (0, 0)),
                  pl.BlockSpec((kdim, tn), lambda i, j: (0, j))],
        out_specs=out_specs,
        out_shape=out_shapes,
        scratch_shapes=[pltpu.VMEM((tm, kdim), BF16)],
        compiler_params=_params(("parallel", "arbitrary")),
        name="norm_mm",
    )(x, g, w16)


def _mm_kernel(*refs, n_pairs, has_res):
    o_ref = refs[-1]
    acc = _dot(refs[0][...], refs[n_pairs][...])
    for i in range(1, n_pairs):
        acc = acc + _dot(refs[i][...], refs[n_pairs + i][...])
    if has_res:
        acc = acc + refs[2 * n_pairs][...]
    o_ref[...] = acc


def _mm(a_list, w_list, res, *, tm, tn):
    m = a_list[0].shape[0]
    n = w_list[0].shape[1]
    in_specs = [pl.BlockSpec((tm, a.shape[1]), lambda i, j: (i, 0)) for a in a_list]
    in_specs += [pl.BlockSpec((w.shape[0], tn), lambda i, j: (0, j)) for w in w_list]
    args = list(a_list) + list(w_list)
    if res is not None:
        in_specs.append(pl.BlockSpec((tm, tn), lambda i, j: (i, j)))
        args.append(res)
    return pl.pallas_call(
        functools.partial(_mm_kernel, n_pairs=len(a_list), has_res=res is not None),
        grid=(m // tm, n // tn),
        in_specs=in_specs,
        out_specs=pl.BlockSpec((tm, tn), lambda i, j: (i, j)),
        out_shape=jax.ShapeDtypeStruct((m, n), F32),
        compiler_params=_params(("parallel", "parallel")),
        name="mm",
    )(*args)


def _attn_kernel(qi_tab, kj_tab, slopes_ref, q_ref, k_ref, v_ref, lq_ref, lk_ref, g_ref, o_ref,
                 m_scr, l_scr, acc_scr, *, t):
    h = pl.program_id(0)
    step = pl.program_id(1)
    qi = qi_tab[step]
    kj = kj_tab[step]

    @pl.when(kj == 0)
    def _():
        m_scr[...] = jnp.full(m_scr.shape, -jnp.inf, F32)
        l_scr[...] = jnp.zeros(l_scr.shape, F32)
        acc_scr[...] = jnp.zeros(acc_scr.shape, F32)

    def update(masked):
        col = lax.broadcasted_iota(jnp.int32, (1, t), 1)
        bias = slopes_ref[h] * (col + (kj - qi) * t).astype(F32)
        q = q_ref[...]
        k = k_ref[...]
        v = v_ref[...]
        if masked:
            causal = (lax.broadcasted_iota(jnp.int32, (t, t), 0)
                      >= lax.broadcasted_iota(jnp.int32, (t, t), 1))
        for j in range(2):
            s = _dot_nt(q[:, j * 128:(j + 1) * 128], k[:, j * 128:(j + 1) * 128]) + bias
            if masked:
                s = jnp.where(causal, s, -jnp.inf)
            m_prev = m_scr[j]
            m_new = jnp.maximum(m_prev, jnp.max(s, axis=-1, keepdims=True))
            alpha = jnp.exp(m_prev - m_new)
            p = jnp.exp(s - m_new[:, :1])
            l_scr[j] = alpha * l_scr[j] + jnp.sum(p, axis=-1, keepdims=True)
            acc_scr[j] = alpha[:, :1] * acc_scr[j] + _dot(p.astype(BF16), v)
            m_scr[j] = m_new

    @pl.when(kj < qi)
    def _():
        update(False)

    @pl.when(kj == qi)
    def _():
        update(True)
        lam = _lambda(lq_ref, lk_ref)
        o0 = acc_scr[0] / l_scr[0][:, :1]
        o1 = acc_scr[1] / l_scr[1][:, :1]
        d = o0 - lam * o1
        ms = jnp.mean(d * d, axis=-1, keepdims=True)
        y = d * lax.rsqrt(ms + NORM_EPS) * g_ref[...]
        o_ref[...] = (y * (1.0 - LAM_INIT)).astype(o_ref.dtype)


def _prompt_attention(q16, k16, v16, lam_q, lam_k, subln_g, *, t):
    l = q16.shape[0]
    nq = l // t
    pairs = [(qi, kj) for qi in range(nq) for kj in range(qi + 1)]
    qi_tab = jnp.asarray([p[0] for p in pairs], jnp.int32)
    kj_tab = jnp.asarray([p[1] for p in pairs], jnp.int32)
    slopes = jnp.asarray(_alibi_slopes(), F32)

    def full(shape):
        return pl.BlockSpec(shape, lambda h, s, qt, kt: tuple(0 for _ in shape))

    return pl.pallas_call(
        functools.partial(_attn_kernel, t=t),
        grid_spec=pltpu.PrefetchScalarGridSpec(
            num_scalar_prefetch=2,
            grid=(ATTN_HEADS, len(pairs)),
            in_specs=[pl.BlockSpec(memory_space=pltpu.SMEM),
                      pl.BlockSpec((t, 256), lambda h, s, qt, kt: (qt[s], h)),
                      pl.BlockSpec((t, 256), lambda h, s, qt, kt: (kt[s], h)),
                      pl.BlockSpec((t, 256), lambda h, s, qt, kt: (kt[s], h)),
                      full((2, 128)), full((2, 128)), full((1, 256))],
            out_specs=pl.BlockSpec((t, 256), lambda h, s, qt, kt: (qt[s], h)),
            scratch_shapes=[pltpu.VMEM((2, t, LANES), F32),
                            pltpu.VMEM((2, t, LANES), F32),
                            pltpu.VMEM((2, t, ATTN_V_DIM), F32)]),
        out_shape=jax.ShapeDtypeStruct((l, ATTN_WIDTH), BF16),
        compiler_params=_params(("parallel", "arbitrary")),
        name="prompt_attn",
    )(qi_tab, kj_tab, slopes, q16, k16, v16, lam_q, lam_k, subln_g)


def _rows_from_lanes(row, n, width):
    return jnp.concatenate([row[:, i * width:(i + 1) * width] for i in range(n)], axis=0)


def _map_major_rows(row):
    w = ATTN_HEAD_DIM
    return jnp.concatenate([row[:, (h * 2 + j) * w:(h * 2 + j + 1) * w]
                            for j in range(2) for h in range(ATTN_HEADS)], axis=0)


def _sample_attn_kernel(pt_ref, q_ref, kn_ref, vn_ref, *rest, n_steps, pages_per_step, page, past):
    del pt_ref
    g_pages = pages_per_step
    k_refs = rest[:g_pages]
    v_refs = rest[g_pages:2 * g_pages]
    slope_ref, lq_ref, lk_ref, g_ref, o_ref, q_scr, m_scr, l_scr, acc_scr = rest[2 * g_pages:]
    p = pl.program_id(1)
    nh = ATTN_HEADS
    rows = page * nh

    @pl.when(p == 0)
    def _():
        q_scr[...] = _map_major_rows(q_ref[0]).astype(BF16)
        m_scr[...] = jnp.full(m_scr.shape, -jnp.inf, F32)
        l_scr[...] = jnp.zeros(l_scr.shape, F32)
        acc_scr[...] = jnp.zeros(acc_scr.shape, F32)

    lane = lax.broadcasted_iota(jnp.int32, (nh, rows), 1)
    own = lane % nh == lax.broadcasted_iota(jnp.int32, (nh, rows), 0)
    slope = slope_ref[...][:, :1]
    for gi in range(g_pages):
        kpos = (p * g_pages + gi) * page + lane // nh
        bias = slope * (past - kpos).astype(F32)
        pes = []
        for j in range(2):
            kj = k_refs[gi][pl.ds(j, rows, stride=2), :].astype(BF16)
            s = _dot_nt(q_scr[j * nh:(j + 1) * nh, :], kj) - bias
            s = jnp.where(own, s, -jnp.inf)
            sl = slice(j * nh, (j + 1) * nh)
            m_prev = m_scr[sl]
            m_new = jnp.maximum(m_prev, jnp.max(s, axis=-1, keepdims=True))
            alpha = jnp.exp(m_prev - m_new)
            pe = jnp.exp(s - m_new[:, :1])
            l_scr[sl] = alpha * l_scr[sl] + jnp.sum(pe, axis=-1, keepdims=True)
            acc_scr[sl] = alpha[:, :1] * acc_scr[sl]
            m_scr[sl] = m_new
            pes.append(pe.astype(BF16))
        acc_scr[...] += _dot(jnp.concatenate(pes, axis=0), v_refs[gi][...].astype(BF16))

    @pl.when(p == n_steps - 1)
    def _():
        kn = _map_major_rows(kn_ref[0]).astype(BF16).astype(F32)
        vn = _rows_from_lanes(vn_ref[0], nh, ATTN_V_DIM).astype(BF16).astype(F32)
        vn2 = jnp.concatenate([vn, vn], axis=0)
        s_self = jnp.sum(q_scr[...].astype(F32) * kn, axis=-1, keepdims=True)
        m_prev = m_scr[...]
        m_new = jnp.maximum(m_prev, s_self)
        alpha = jnp.exp(m_prev - m_new)
        p_self = jnp.exp(s_self - m_new)
        l_fin = alpha * l_scr[...] + p_self
        acc = alpha[:, :1] * acc_scr[...] + p_self[:, :1] * vn2
        n = acc / l_fin[:, :1]
        d = n[0:nh, :] - _lambda(lq_ref, lk_ref) * n[nh:2 * nh, :]
        ms = jnp.mean(d * d, axis=-1, keepdims=True)
        o_ref[0] = d * lax.rsqrt(ms + NORM_EPS) * g_ref[...] * (1.0 - LAM_INIT)


def _sample_attention(q, k_new, v_new, cache_k, cache_v, page_table, lam_q, lam_k, subln_g):
    b = q.shape[0]
    page = cache_k.shape[1]
    n_pages = page_table.shape[1]
    width = q.shape[1]
    ck = cache_k.reshape(-1, ATTN_HEAD_DIM)
    cv = cache_v.reshape(-1, ATTN_V_DIM)
    g_pages = math.gcd(n_pages, 4)
    n_steps = n_pages // g_pages
    slope_tile = jnp.asarray(np.repeat(_alibi_slopes()[:, None], LANES, axis=1), F32)
    row_spec = pl.BlockSpec((1, 1, width), lambda i, p, pt: (i, 0, 0))

    def page_spec(rows, lanes, gi):
        return pl.BlockSpec((rows, lanes), lambda i, p, pt: (pt[i * n_pages + p * g_pages + gi], 0))

    def full(shape):
        return pl.BlockSpec(shape, lambda i, p, pt: tuple(0 for _ in shape))

    out = pl.pallas_call(
        functools.partial(_sample_attn_kernel, n_steps=n_steps, pages_per_step=g_pages, page=page,
                          past=n_pages * page),
        grid_spec=pltpu.PrefetchScalarGridSpec(
            num_scalar_prefetch=1,
            grid=(b, n_steps),
            in_specs=[row_spec, row_spec, row_spec]
            + [page_spec(page * ATTN_HEADS * 2, ATTN_HEAD_DIM, gi) for gi in range(g_pages)]
            + [page_spec(page * ATTN_HEADS, ATTN_V_DIM, gi) for gi in range(g_pages)]
            + [full((ATTN_HEADS, LANES)), full((2, 128)), full((2, 128)), full((1, ATTN_V_DIM))],
            out_specs=pl.BlockSpec((1, ATTN_HEADS, ATTN_V_DIM), lambda i, p, pt: (i, 0, 0)),
            scratch_shapes=[pltpu.VMEM((2 * ATTN_HEADS, ATTN_HEAD_DIM), BF16),
                            pltpu.VMEM((2 * ATTN_HEADS, LANES), F32),
                            pltpu.VMEM((2 * ATTN_HEADS, LANES), F32),
                            pltpu.VMEM((2 * ATTN_HEADS, ATTN_V_DIM), F32)]),
        out_shape=jax.ShapeDtypeStruct((b, ATTN_HEADS, ATTN_V_DIM), F32),
        compiler_params=_params(("parallel", "arbitrary")),
        name="sample_attn",
    )(page_table.reshape(-1), q.reshape(b, 1, width), k_new.reshape(b, 1, width),
      v_new.reshape(b, 1, width), *([ck] * g_pages), *([cv] * g_pages), slope_tile, lam_q, lam_k, subln_g)
    return out.reshape(b, ATTN_WIDTH)


def _ssd_prompt_kernel(xbc_ref, z_ref, dtr_ref, cw_ref, cb_ref, dtb_ref, alog_ref, dexp_ref, ng_ref,
                       e_ref, y_ref, hfin_ref, buf, xc, ht, *, n_chunks):
    c = pl.program_id(0)
    q = SSM_CHUNK

    @pl.when(c == 0)
    def _():
        buf[0:8, :] = jnp.zeros((8, SSM_CONV_DIM), F32)
        ht[...] = jnp.zeros(ht.shape, F32)

    buf[8:8 + q, :] = xbc_ref[...]
    blk = 512
    for cbk in range(SSM_CONV_DIM // blk):
        sl = slice(cbk * blk, (cbk + 1) * blk)
        acc = cb_ref[:, sl]
        for j in range(SSM_CONV):
            acc = acc + cw_ref[j:j + 1, sl] * buf[5 + j:5 + j + q, sl]
        xc[:, sl] = _silu(acc)
    buf[0:8, :] = buf[q:q + 8, :]

    dt = _softplus(dtr_ref[...] + dtb_ref[...])
    a = dt * (-jnp.exp(alog_ref[...]))
    ri = lax.broadcasted_iota(jnp.int32, (q, q), 0)
    ci = lax.broadcasted_iota(jnp.int32, (q, q), 1)
    causal = ri >= ci
    tril = jnp.where(causal, 1.0, 0.0).astype(BF16)
    a_cum = _dot_exact_lhs(tril, a)
    a_cum_t = a_cum.T
    e16 = e_ref[...]
    ac_exp = _dot_exact_rhs(a_cum, e16)
    dt_exp = _dot_exact_rhs(dt, e16)
    ac_last = ac_exp[q - 1:q, :]
    xs = xc[:, 0:SSM_WIDTH]
    xdt = xs * dt_exp
    xw = xdt * jnp.exp(ac_last - ac_exp)
    ea = jnp.exp(ac_exp)
    cdec = jnp.exp(ac_last)
    lane_g = lax.broadcasted_iota(jnp.int32, (q, GROUP_WIDTH), 1)

    for g in range(SSM_GROUPS):
        gs = slice(g * GROUP_WIDTH, (g + 1) * GROUP_WIDTH)
        bm = xc[:, SSM_WIDTH + g * SSM_STATE:SSM_WIDTH + (g + 1) * SSM_STATE]
        cm = xc[:, SSM_WIDTH + (SSM_GROUPS + g) * SSM_STATE:SSM_WIDTH + (SSM_GROUPS + g + 1) * SSM_STATE]
        bm16 = bm.astype(BF16)
        cm16 = cm.astype(BF16)
        cbm = _dot_nt(cm16, bm16)
        xdt16 = xdt[:, gs].astype(BF16)
        yg = jnp.zeros((q, GROUP_WIDTH), F32)
        for r in range(SSM_HEADS // SSM_GROUPS):
            hd = g * (SSM_HEADS // SSM_GROUPS) + r
            seg = a_cum[:, hd:hd + 1] - a_cum_t[hd:hd + 1, :]
            dec = jnp.exp(jnp.where(causal, seg, -jnp.inf))
            yr = _dot((cbm * dec).astype(BF16), xdt16)
            yg = jnp.where(lane_g // SSM_HEAD_DIM == r, yr, yg)
        h_in = ht[:, gs]
        y_off = _dot(cm16, h_in.astype(BF16)) * ea[:, gs]
        st = _dot(bm.T.astype(BF16), xw[:, gs].astype(BF16))
        ht[:, gs] = cdec[:, gs] * h_in + st
        y = yg + y_off + dexp_ref[:, gs] * xs[:, gs]
        gated = y * _silu(z_ref[:, gs])
        ms = jnp.mean(gated * gated, axis=-1, keepdims=True)
        y_ref[:, gs] = (gated * lax.rsqrt(ms + NORM_EPS) * ng_ref[:, gs]).astype(y_ref.dtype)

    @pl.when(c == n_chunks - 1)
    def _():
        hfin_ref[...] = ht[...].T


def _head_expand_matrix():
    e = np.zeros((LANES, SSM_WIDTH), np.float32)
    for hd in range(SSM_HEADS):
        e[hd, hd * SSM_HEAD_DIM:(hd + 1) * SSM_HEAD_DIM] = 1.0
    return jnp.asarray(e, BF16)


def _pad_lanes(v):
    return jnp.pad(v.reshape(1, -1), ((0, 0), (0, LANES - v.size)))


def _ssd_prompt(xbc, z, dt_raw, conv_w, conv_b, dt_bias, a_log, d_skip, norm_g):
    l = xbc.shape[0]
    n_chunks = l // SSM_CHUNK
    q = SSM_CHUNK

    def full(shape):
        return pl.BlockSpec(shape, lambda c: tuple(0 for _ in shape))

    y16, hfin = pl.pallas_call(
        functools.partial(_ssd_prompt_kernel, n_chunks=n_chunks),
        grid=(n_chunks,),
        in_specs=[pl.BlockSpec((q, SSM_CONV_DIM), lambda c: (c, 0)),
                  pl.BlockSpec((q, SSM_WIDTH), lambda c: (c, 0)),
                  pl.BlockSpec((q, LANES), lambda c: (c, 0)),
                  full((SSM_CONV, SSM_CONV_DIM)), full((1, SSM_CONV_DIM)),
                  full((1, LANES)), full((1, LANES)), full((1, SSM_WIDTH)), full((1, SSM_WIDTH)),
                  full((LANES, SSM_WIDTH))],
        out_specs=[pl.BlockSpec((q, SSM_WIDTH), lambda c: (c, 0)),
                   full((SSM_WIDTH, SSM_STATE))],
        out_shape=[jax.ShapeDtypeStruct((l, SSM_WIDTH), BF16),
                   jax.ShapeDtypeStruct((SSM_WIDTH, SSM_STATE), F32)],
        scratch_shapes=[pltpu.VMEM((q + 8, SSM_CONV_DIM), F32),
                        pltpu.VMEM((q, SSM_CONV_DIM), F32),
                        pltpu.VMEM((SSM_STATE, SSM_WIDTH), F32)],
        compiler_params=_params(("arbitrary",)),
        name="ssd_prompt",
    )(xbc, z, dt_raw, conv_w, conv_b.reshape(1, -1), _pad_lanes(dt_bias), _pad_lanes(a_log),
      jnp.repeat(d_skip, SSM_HEAD_DIM).reshape(1, -1), norm_g.reshape(1, -1), _head_expand_matrix())
    return y16, hfin


def _ssd_sample_pre_kernel(xbc_ref, dtr_ref, sc_ref, cw_ref, cb_ref, dtb_ref, aexp_ref, e_ref,
                           xc_ref, cn_ref, xdt_t_ref, dec_t_ref):
    xbc = xbc_ref[...]
    acc = cb_ref[...] + cw_ref[SSM_CONV - 1:SSM_CONV, :] * xbc
    for j in range(SSM_CONV - 1):
        acc = acc + cw_ref[j:j + 1, :] * sc_ref[j]
    xcv = _silu(acc)
    xc_ref[...] = xcv
    cn_ref[0] = sc_ref[1]
    cn_ref[1] = sc_ref[2]
    cn_ref[2] = xbc
    dt = _softplus(dtr_ref[...] + dtb_ref[...])
    dt_exp = _dot_exact_rhs(dt, e_ref[...])
    dec_t_ref[...] = jnp.exp(dt_exp * (-jnp.exp(aexp_ref[...]))).T
    xdt_t_ref[...] = (xcv[:, 0:SSM_WIDTH] * dt_exp).T


def _ssd_sample_state_kernel(xdt_t_ref, dec_t_ref, xc_ref, s_ref, so_ref, yt_ref):
    b = pl.program_id(0)
    nb = xdt_t_ref.shape[1]
    onehot = jnp.where(lax.broadcasted_iota(jnp.int32, (nb, SSM_STATE), 0) == b, 1.0, 0.0).astype(BF16)
    lane = lax.broadcasted_iota(jnp.int32, (GROUP_WIDTH, nb), 1)

    @pl.when(b == 0)
    def _():
        yt_ref[...] = jnp.zeros(yt_ref.shape, F32)

    xc_row = xc_ref[pl.ds(b, 1), :]
    for g in range(SSM_GROUPS):
        gs = slice(g * GROUP_WIDTH, (g + 1) * GROUP_WIDTH)
        xb = _dot_exact_rhs(xdt_t_ref[gs, :], onehot)
        db = _dot_exact_rhs(dec_t_ref[gs, :], onehot)
        bm = xc_row[:, SSM_WIDTH + g * SSM_STATE:SSM_WIDTH + (g + 1) * SSM_STATE]
        cm = xc_row[:, SSM_WIDTH + (SSM_GROUPS + g) * SSM_STATE:SSM_WIDTH + (SSM_GROUPS + g + 1) * SSM_STATE]
        hn = db * s_ref[gs, :] + xb * bm
        so_ref[gs, :] = hn
        ycol = jnp.sum(hn * cm, axis=-1, keepdims=True)
        yt_ref[gs, :] = jnp.where(lane == b, ycol, yt_ref[gs, :])


def _ssd_sample_post_kernel(yt_ref, xc_ref, z_ref, dexp_ref, ng_ref, o_ref):
    yraw = yt_ref[...].T
    for g in range(SSM_GROUPS):
        gs = slice(g * GROUP_WIDTH, (g + 1) * GROUP_WIDTH)
        y = yraw[:, gs] + dexp_ref[:, gs] * xc_ref[:, gs]
        gated = y * _silu(z_ref[:, gs])
        ms = jnp.mean(gated * gated, axis=-1, keepdims=True)
        o_ref[:, gs] = gated * lax.rsqrt(ms + NORM_EPS) * ng_ref[:, gs]


def _ssd_sample(xbc, z, dt_raw, state_conv, state_ssm, conv_w, conv_b, dt_bias, a_log, d_skip, norm_g):
    nb = xbc.shape[0]
    rows = SSM_HEADS * SSM_HEAD_DIM
    sc = jnp.swapaxes(state_conv, 0, 1)
    st = state_ssm.reshape(nb * rows, SSM_STATE)
    xc, cn, xdt_t, dec_t = pl.pallas_call(
        _ssd_sample_pre_kernel,
        out_shape=[jax.ShapeDtypeStruct((nb, SSM_CONV_DIM), F32),
                   jax.ShapeDtypeStruct((SSM_CONV - 1, nb, SSM_CONV_DIM), F32),
                   jax.ShapeDtypeStruct((SSM_WIDTH, nb), F32),
                   jax.ShapeDtypeStruct((SSM_WIDTH, nb), F32)],
        compiler_params=pltpu.CompilerParams(vmem_limit_bytes=VMEM_LIMIT),
        name="ssd_sample_pre",
    )(xbc, dt_raw, sc, conv_w, conv_b.reshape(1, -1), _pad_lanes(dt_bias),
      jnp.repeat(a_log, SSM_HEAD_DIM).reshape(1, -1), _head_expand_matrix())

    def full(shape):
        return pl.BlockSpec(shape, lambda i: tuple(0 for _ in shape))

    so, yt = pl.pallas_call(
        _ssd_sample_state_kernel,
        grid=(nb,),
        in_specs=[full((SSM_WIDTH, nb)), full((SSM_WIDTH, nb)), full((nb, SSM_CONV_DIM)),
                  pl.BlockSpec((rows, SSM_STATE), lambda i: (i, 0))],
        out_specs=[pl.BlockSpec((rows, SSM_STATE), lambda i: (i, 0)), full((SSM_WIDTH, nb))],
        out_shape=[jax.ShapeDtypeStruct(st.shape, F32),
                   jax.ShapeDtypeStruct((SSM_WIDTH, nb), F32)],
        compiler_params=_params(("arbitrary",)),
        name="ssd_sample_state",
    )(xdt_t, dec_t, xc, st)
    y = pl.pallas_call(
        _ssd_sample_post_kernel,
        out_shape=jax.ShapeDtypeStruct((nb, SSM_WIDTH), F32),
        compiler_params=pltpu.CompilerParams(vmem_limit_bytes=VMEM_LIMIT),
        name="ssd_sample_post",
    )(yt, xc, z, jnp.repeat(d_skip, SSM_HEAD_DIM).reshape(1, -1), norm_g.reshape(1, -1))
    return y, jnp.swapaxes(cn, 0, 1), so.reshape(state_ssm.shape)


def _peer_query_kernel(h_ref, g_ref, wqt_ref, xnt_ref, qt_ref, xnt_scr):
    j = pl.program_id(1)

    @pl.when(j == 0)
    def _():
        x = h_ref[...]
        ms = jnp.mean(x * x, axis=-1, keepdims=True)
        xn = x * lax.rsqrt(ms + NORM_EPS) * g_ref[...]
        xnt = xn.T.astype(BF16)
        xnt_scr[...] = xnt
        xnt_ref[...] = xnt

    qt_ref[...] = _dot(wqt_ref[...], xnt_scr[...])


def _peer_query(h, g, wqt16, *, tm, tr):
    m, d = h.shape
    nq = wqt16.shape[0]
    return pl.pallas_call(
        _peer_query_kernel,
        grid=(m // tm, nq // tr),
        in_specs=[pl.BlockSpec((tm, d), lambda i, j: (i, 0)),
                  pl.BlockSpec((1, d), lambda i, j: (0, 0)),
                  pl.BlockSpec((tr, d), lambda i, j: (j, 0))],
        out_specs=[pl.BlockSpec((d, tm), lambda i, j: (0, i)),
                   pl.BlockSpec((tr, tm), lambda i, j: (j, i))],
        out_shape=[jax.ShapeDtypeStruct((d, m), BF16),
                   jax.ShapeDtypeStruct((nq, m), F32)],
        scratch_shapes=[pltpu.VMEM((d, tm), BF16)],
        compiler_params=_params(("parallel", "arbitrary")),
        name="peer_query",
    )(h, g, wqt16)


def _peer_retrieve_kernel(qt_ref, keys_ref, s1_ref, c1_ref, thr1_ref, eq1_ref, s2_ref, e2_ref, thr2_ref):
    tb = qt_ref.shape[1]
    nkeys = PEER_KEYS
    iota_n = lax.broadcasted_iota(jnp.int32, (nkeys, tb), 0)
    npair = PEER_TOPK * PEER_TOPK
    iota_p = lax.broadcasted_iota(jnp.int32, (npair, tb), 0)
    for h in range(PEER_HEADS):
        tops, scores, ranks = [], [], []
        for j in range(2):
            hj = 2 * h + j
            s = _dot(keys_ref[hj], qt_ref[hj * PEER_HALF:(hj + 1) * PEER_HALF, :].astype(BF16))
            cur = s
            rank = jnp.full((nkeys, tb), float(PEER_TOPK), F32)
            vals = []
            for k in range(PEER_TOPK):
                m = jnp.max(cur, axis=0, keepdims=True)
                first = jnp.min(jnp.where(cur == m, iota_n, nkeys), axis=0, keepdims=True)
                hit = iota_n == first
                rank = jnp.where(hit, float(k), rank)
                cur = jnp.where(hit, -jnp.inf, cur)
                vals.append(m)
            tops.append(vals)
            scores.append(s)
            ranks.append(rank)
        v2 = jnp.concatenate(tops[1], axis=0)
        cand = jnp.concatenate([tops[0][ka] + v2 for ka in range(PEER_TOPK)], axis=0)
        picked = []
        pos = None
        for k in range(PEER_TOPK):
            m = jnp.max(cand, axis=0, keepdims=True)
            pos = jnp.min(jnp.where(cand == m, iota_p, npair), axis=0, keepdims=True)
            cand = jnp.where(iota_p == pos, -jnp.inf, cand)
            picked.append(m)
        tau = picked[-1]
        tau_up = jnp.full_like(tau, jnp.inf)
        z = jnp.zeros_like(tau)
        for m in picked:
            tau_up = jnp.minimum(tau_up, jnp.where(m > tau, m, jnp.inf))
            z = z + jnp.exp(m - picked[0])
        posf = pos.astype(F32)
        ra = jnp.floor(posf / PEER_TOPK)
        rb = posf - ra * PEER_TOPK
        m1 = tops[0][0]
        m2 = tops[1][0]
        zsum = z * jnp.exp(picked[0] - (m1 + m2))
        s1_ref[h, 0] = scores[0]
        c1_ref[h, 0] = jnp.exp(scores[0] - m1) / zsum
        thr1_ref[h, 0] = jnp.where(ranks[0] < ra, tau, tau_up)
        eq1_ref[h, 0] = jnp.where(ranks[0] == ra, 1.0, 0.0)
        s2_ref[h, 0] = scores[1]
        e2_ref[h, 0] = jnp.exp(scores[1] - m2)
        thr2_ref[h, 0] = jnp.where(ranks[1] <= rb, tau, tau_up)


def _peer_retrieve(qt, keys16):
    m = qt.shape[1]
    tb = LANES
    nhj = 2 * PEER_HEADS
    spec = pl.BlockSpec((PEER_HEADS, 1, PEER_KEYS, tb), lambda i: (0, i, 0, 0))
    shape = jax.ShapeDtypeStruct((PEER_HEADS, m // tb, PEER_KEYS, tb), F32)
    return pl.pallas_call(
        _peer_retrieve_kernel,
        grid=(m // tb,),
        in_specs=[pl.BlockSpec((qt.shape[0], tb), lambda i: (0, i)),
                  pl.BlockSpec((nhj, PEER_KEYS, PEER_HALF), lambda i: (0, 0, 0))],
        out_specs=[spec] * 7,
        out_shape=[shape] * 7,
        compiler_params=_params(("parallel",)),
        name="peer_retrieve",
    )(qt, keys16)


def _gelu(x):
    return 0.5 * x * (1.0 + lax.erf(x * (2.0 ** -0.5)))


def _peer_expert_kernel(xnt_ref, s1_ref, c1_ref, thr1_ref, eq1_ref, s2_ref, e2_ref, thr2_ref,
                        u_ref, vt_ref, o_ref, act0, act1, hd, *, n_eblk, a_per_blk):
    e = pl.program_id(1)
    nk = PEER_KEYS
    tt = xnt_ref.shape[1]

    def u_dot(dst):
        dst[...] = _dot(u_ref[...], xnt_ref[...])

    def mask_gelu(src, ai):
        a = (e - 1) * a_per_blk + ai
        row = pl.ds(a, 1)
        for tl in range(tt // LANES):
            ls = slice(tl * LANES, (tl + 1) * LANES)
            w = jnp.zeros((nk, LANES), F32)
            for h in range(PEER_HEADS):
                thr = jnp.where(eq1_ref[h, tl, row, :] > 0.5, thr2_ref[h, tl], thr1_ref[h, tl, row, :])
                pair = s2_ref[h, tl] + s1_ref[h, tl, row, :]
                w = w + jnp.where(pair >= thr, e2_ref[h, tl] * c1_ref[h, tl, row, :], 0.0)
            hd[ai * nk:(ai + 1) * nk, ls] = (_gelu(src[ai * nk:(ai + 1) * nk, ls]) * w).astype(BF16)

    def skewed(dst, src):
        u_dot(dst)
        for ai in range(a_per_blk):
            mask_gelu(src, ai)
        o_ref[...] += _dot(vt_ref[...], hd[...])

    @pl.when(e == 0)
    def _():
        o_ref[...] = jnp.zeros(o_ref.shape, F32)
        u_dot(act0)

    @pl.when((e > 0) & (e < n_eblk) & (e % 2 == 1))
    def _():
        skewed(act1, act0)

    @pl.when((e > 0) & (e < n_eblk) & (e % 2 == 0))
    def _():
        skewed(act0, act1)

    @pl.when(e == n_eblk)
    def _():
        src = act1 if (n_eblk - 1) % 2 == 1 else act0
        for ai in range(a_per_blk):
            mask_gelu(src, ai)
        o_ref[...] += _dot(vt_ref[...], hd[...])


def _peer_experts(xnt, tiles, u16, vt16, *, tt, eb):
    d, m = xnt.shape
    n_exp = u16.shape[0]
    n_eblk = n_exp // eb
    once = pl.Buffered(1)
    tile_spec = pl.BlockSpec((PEER_HEADS, tt // LANES, PEER_KEYS, LANES), lambda i, e: (0, i, 0, 0),
                             pipeline_mode=once)
    return pl.pallas_call(
        functools.partial(_peer_expert_kernel, n_eblk=n_eblk, a_per_blk=eb // PEER_KEYS),
        grid=(m // tt, n_eblk + 1),
        in_specs=[pl.BlockSpec((d, tt), lambda i, e: (0, i), pipeline_mode=once)]
        + [tile_spec] * 7
        + [pl.BlockSpec((eb, d), lambda i, e: (jnp.minimum(e, n_eblk - 1), 0)),
           pl.BlockSpec((d, eb), lambda i, e: (0, jnp.maximum(e - 1, 0)))],
        out_specs=pl.BlockSpec((d, tt), lambda i, e: (0, i), pipeline_mode=once),
        out_shape=jax.ShapeDtypeStruct((d, m), F32),
        scratch_shapes=[pltpu.VMEM((eb, tt), F32), pltpu.VMEM((eb, tt), F32), pltpu.VMEM((eb, tt), BF16)],
        compiler_params=_params(("parallel", "arbitrary")),
        name="peer_experts",
    )(xnt, *tiles, u16, vt16)


def _final_kernel(h_ref, ft_ref, g_ref, o_ref):
    x = h_ref[...] + ft_ref[...].T
    ms = jnp.mean(x * x, axis=-1, keepdims=True)
    o_ref[...] = x * lax.rsqrt(ms + NORM_EPS) * g_ref[...]


def _final_norm(h, ffn_t, g, *, tm):
    m, d = h.shape
    row = pl.BlockSpec((tm, d), lambda i: (i, 0))
    return pl.pallas_call(
        _final_kernel,
        grid=(m // tm,),
        in_specs=[row, pl.BlockSpec((d, tm), lambda i: (0, i)), pl.BlockSpec((1, d), lambda i: (0, 0))],
        out_specs=row,
        out_shape=jax.ShapeDtypeStruct((m, d), F32),
        compiler_params=_params(("parallel",)),
        name="final_norm",
    )(h, ffn_t, g)


def _tile(m, pref):
    return pref if m % pref == 0 else m


def _in_proj(x, g_mix, w16, wdt16):
    m = x.shape[0]
    tm = _tile(m, 512)
    qw = 2 * ATTN_HEADS * ATTN_HEAD_DIM
    segs = [(0, qw, "bf16", ATTN_SCALE), (qw, qw, "both", 1.0), (2 * qw, ATTN_WIDTH, "both", 1.0),
            (2 * qw + ATTN_WIDTH, SSM_WIDTH, "f32", 1.0),
            (2 * qw + ATTN_WIDTH + SSM_WIDTH, SSM_CONV_DIM, "f32", 1.0)]
    q16, k32, k16, v32, v16, z32, xbc32, xn16 = _norm_mm(x, g_mix, w16, segs, tm=tm, tn=512, emit_xn=True)
    dt_raw = _mm([xn16], [wdt16], None, tm=tm, tn=LANES)
    return q16, k32, k16, v32, v16, z32, xbc32, dt_raw


def _peer_and_final(h, g_ffn, wqt16, keys16, u16, vt16, g_final):
    m = h.shape[0]
    xnt, qt = _peer_query(h, g_ffn, wqt16, tm=_tile(m, 256), tr=512)
    tiles = _peer_retrieve(qt, keys16)
    ffn_t = _peer_experts(xnt, tiles, u16, vt16, tt=_tile(m, 512), eb=512)
    return _final_norm(h, ffn_t, g_final, tm=_tile(m, 256))


def kernel(x_prompt, x_sample, cache_k, cache_v, state_conv, state_ssm, page_table, g_mix, w_in, lambda_q, lambda_k, subln_g, conv_w, conv_b, dt_bias, a_log, d_skip, ssm_norm_g, w_out, g_ffn, peer_wq, peer_keys, peer_u, peer_v, g_final):
    assert w_in.shape[0] == 1, "single trunk layer"
    bsz, seq, dm = x_prompt.shape
    nb = x_sample.shape[0]
    assert bsz == 1 and x_sample.shape[1] == 1
    main_cols = w_in.shape[2] - SSM_HEADS
    w16 = w_in[0].astype(BF16)
    wdt16 = jnp.pad(w_in[0][:, main_cols:], ((0, 0), (0, LANES - SSM_HEADS))).astype(BF16)
    wo16 = w_out[0].astype(BF16)
    wqt16 = peer_wq[0].T.astype(BF16)
    keys16 = peer_keys[0].reshape(2 * PEER_HEADS, PEER_KEYS, PEER_HALF).astype(BF16)
    u16 = peer_u[0].astype(BF16)
    vt16 = peer_v[0].T.astype(BF16)
    gm = g_mix[0].reshape(1, -1)
    gf = g_ffn[0].reshape(1, -1)
    gfin = g_final.reshape(1, -1)
    sg = subln_g[0].reshape(1, -1)
    lq, lk = lambda_q[0], lambda_k[0]

    xp = x_prompt.reshape(seq, dm)
    q16, k32, k16, v32, v16, z32, xbc32, dtr = _in_proj(xp, gm, w16, wdt16)
    attn16 = _prompt_attention(q16, k16, v16, lq, lk, sg, t=_tile(seq, 512))
    y16, hfin = _ssd_prompt(xbc32, z32, dtr, conv_w[0], conv_b[0], dt_bias[0], a_log[0], d_skip[0],
                            ssm_norm_g[0])
    hp = _mm([attn16, y16], [wo16[:ATTN_WIDTH], wo16[ATTN_WIDTH:]], xp, tm=_tile(seq, 512), tn=512)
    y_prompt = _peer_and_final(hp, gf, wqt16, keys16, u16, vt16, gfin)

    xs = x_sample.reshape(nb, dm)
    sq16, sk32, _, sv32, _, sz32, sxbc32, sdtr = _in_proj(xs, gm, w16, wdt16)
    sattn = _sample_attention(sq16.astype(F32), sk32, sv32, cache_k[0], cache_v[0], page_table, lq, lk, sg)
    sy, conv_s, ssm_s = _ssd_sample(sxbc32, sz32, sdtr, state_conv[0], state_ssm[0], conv_w[0], conv_b[0],
                                    dt_bias[0], a_log[0], d_skip[0], ssm_norm_g[0])
    hs = _mm([sattn.astype(BF16), sy.astype(BF16)], [wo16[:ATTN_WIDTH], wo16[ATTN_WIDTH:]], xs,
             tm=nb, tn=512)
    y_sample = _peer_and_final(hs, gf, wqt16, keys16, u16, vt16, gfin)

    return (y_prompt.reshape(bsz, seq, dm),
            y_sample.reshape(nb, 1, dm),
            k32.reshape(1, bsz, seq, ATTN_HEADS, 2, ATTN_HEAD_DIM),
            v32.reshape(1, bsz, seq, ATTN_HEADS, ATTN_V_DIM),
            xbc32[seq - (SSM_CONV - 1):].reshape(1, bsz, SSM_CONV - 1, SSM_CONV_DIM),
            hfin.reshape(1, bsz, SSM_HEADS, SSM_HEAD_DIM, SSM_STATE),
            sk32.reshape(1, nb, 1, ATTN_HEADS, 2, ATTN_HEAD_DIM),
            sv32.reshape(1, nb, 1, ATTN_HEADS, ATTN_V_DIM),
            conv_s.reshape(1, nb, SSM_CONV - 1, SSM_CONV_DIM),
            ssm_s.reshape(1, nb, SSM_HEADS, SSM_HEAD_DIM, SSM_STATE))
```

```python
import functools
import math

import jax
import jax.numpy as jnp
import numpy as np
from jax import lax
from jax.experimental import pallas as pl
from jax.experimental.pallas import tpu as pltpu

F32 = jnp.float32
BF16 = jnp.bfloat16

ATTN_HEADS = 8
ATTN_HEAD_DIM = 128
ATTN_V_DIM = 256
ATTN_WIDTH = ATTN_HEADS * ATTN_V_DIM
SSM_WIDTH = 2048
SSM_HEAD_DIM = 64
SSM_HEADS = 32
SSM_GROUPS = 8
SSM_STATE = 128
SSM_CONV = 4
SSM_CHUNK = 128
SSM_CONV_DIM = SSM_WIDTH + 2 * SSM_GROUPS * SSM_STATE
GROUP_WIDTH = SSM_WIDTH // SSM_GROUPS
PEER_HEADS = 8
PEER_KEYS = 128
PEER_HALF = 128
PEER_TOPK = 16
NORM_EPS = 1e-6
LAM_INIT = 0.8 - 0.6 * math.exp(-0.3 * 0)
ATTN_SCALE = ATTN_HEAD_DIM ** -0.5
LANES = 128
VMEM_LIMIT = 56 * 1024 * 1024


def _params(sem):
    return pltpu.CompilerParams(dimension_semantics=sem, vmem_limit_bytes=VMEM_LIMIT)


def _split3(x):
    x1 = x.astype(BF16)
    r1 = x - x1.astype(F32)
    x2 = r1.astype(BF16)
    x3 = (r1 - x2.astype(F32)).astype(BF16)
    return x1, x2, x3


def _dot(a, b):
    return jnp.dot(a, b, preferred_element_type=F32)


def _dot_nt(a, b):
    return lax.dot_general(a, b, (((1,), (1,)), ((), ())), preferred_element_type=F32)


def _dot_exact_rhs(x, w16):
    x1, x2, x3 = _split3(x)
    return _dot(x1, w16) + _dot(x2, w16) + _dot(x3, w16)


def _dot_exact_lhs(w16, x):
    x1, x2, x3 = _split3(x)
    return _dot(w16, x1) + _dot(w16, x2) + _dot(w16, x3)


def _silu(x):
    return x / (1.0 + jnp.exp(-x))


def _softplus(x):
    return jnp.maximum(x, 0.0) + jnp.log1p(jnp.exp(-jnp.abs(x)))


def _lambda(lq_ref, lk_ref):
    e = jnp.exp(jnp.sum(lq_ref[...] * lk_ref[...], axis=-1, keepdims=True))
    return e[0:1, :] - e[1:2, :] + LAM_INIT


def _alibi_slopes():
    return 2.0 ** (-8.0 * np.arange(1, ATTN_HEADS + 1) / ATTN_HEADS)


def _norm_mm_kernel(x_ref, g_ref, w_ref, *rest, segs, emit_xn):
    outs = list(rest[:-1])
    xn_scr = rest[-1]
    j = pl.program_id(1)

    @pl.when(j == 0)
    def _():
        x = x_ref[...]
        ms = jnp.mean(x * x, axis=-1, keepdims=True)
        xn = (x * lax.rsqrt(ms + NORM_EPS) * g_ref[...]).astype(BF16)
        xn_scr[...] = xn
        if emit_xn:
            outs[-1][...] = xn

    acc = _dot(xn_scr[...], w_ref[...])
    k = 0
    for (start, nblk, scale, kinds) in segs:
        refs = [(outs[k + i], kind) for i, kind in enumerate(kinds)]
        k += len(kinds)

        @pl.when((j >= start) & (j < start + nblk))
        def _(refs=refs, scale=scale):
            val = acc if scale == 1.0 else acc * scale
            for r, kind in refs:
                r[...] = (val.T if kind.endswith("T") else val).astype(r.dtype)


def _norm_mm(x, g, w16, segs, *, tm, tn, emit_xn):
    m, kdim = x.shape
    nblocks = sum(wd for _, wd, _, _ in segs) // tn
    bsegs = [(s // tn, wd // tn, scale, kinds) for s, wd, scale, kinds in segs]
    out_shapes, out_specs = [], []
    for (sb, nb, _, kinds), (_, wd, _, _) in zip(bsegs, segs):
        def imap(i, j, sb=sb, nb=nb):
            return (i, jnp.clip(j - sb, 0, nb - 1))

        def imap_t(i, j, sb=sb, nb=nb):
            return (jnp.clip(j - sb, 0, nb - 1), i)
        for kind in kinds:
            dt = F32 if kind == "f32" else BF16
            if kind.endswith("T"):
                out_shapes.append(jax.ShapeDtypeStruct((wd, m), dt))
                out_specs.append(pl.BlockSpec((tn, tm), imap_t))
            else:
                out_shapes.append(jax.ShapeDtypeStruct((m, wd), dt))
                out_specs.append(pl.BlockSpec((tm, tn), imap))
    if emit_xn:
        out_shapes.append(jax.ShapeDtypeStruct((m, kdim), BF16))
        out_specs.append(pl.BlockSpec((tm, kdim), lambda i, j: (i, 0)))
    return pl.pallas_call(
        functools.partial(_norm_mm_kernel, segs=bsegs, emit_xn=emit_xn),
        grid=(m // tm, nblocks),
        in_specs=[pl.BlockSpec((tm, kdim), lambda i, j: (i, 0), pipeline_mode=pl.Buffered(1)),
                  pl.BlockSpec((1, kdim), lambda i, j: (0, 0)),
                  pl.BlockSpec((kdim, tn), lambda i, j: (0, j))],
        out_specs=out_specs,
        out_shape=out_shapes,
        scratch_shapes=[pltpu.VMEM((tm, kdim), BF16)],
        compiler_params=_params(("parallel", "arbitrary")),
        name="norm_mm",
    )(x, g, w16)


def _mm_kernel(*refs, n_pairs, has_res):
    o_ref = refs[-1]
    acc = _dot(refs[0][...], refs[n_pairs][...])
    for i in range(1, n_pairs):
        acc = acc + _dot(refs[i][...], refs[n_pairs + i][...])
    if has_res:
        acc = acc + refs[2 * n_pairs][...]
    o_ref[...] = acc


def _mm(a_list, w_list, res, *, tm, tn):
    m = a_list[0].shape[0]
    n = w_list[0].shape[1]
    in_specs = [pl.BlockSpec((tm, a.shape[1]), lambda i, j: (i, 0)) for a in a_list]
    in_specs += [pl.BlockSpec((w.shape[0], tn), lambda i, j: (0, j)) for w in w_list]
    args = list(a_list) + list(w_list)
    if res is not None:
        in_specs.append(pl.BlockSpec((tm, tn), lambda i, j: (i, j)))
        args.append(res)
    return pl.pallas_call(
        functools.partial(_mm_kernel, n_pairs=len(a_list), has_res=res is not None),
        grid=(m // tm, n // tn),
        in_specs=in_specs,
        out_specs=pl.BlockSpec((tm, tn), lambda i, j: (i, j)),
        out_shape=jax.ShapeDtypeStruct((m, n), F32),
        compiler_params=_params(("parallel", "parallel")),
        name="mm",
    )(*args)


def _attn_kernel(qi_tab, kj_tab, slopes_ref, qt_ref, k_ref, vt_ref, lq_ref, lk_ref, g_ref, o_ref,
                 m_scr, l_scr, acc_scr, *, t):
    h = pl.program_id(0)
    step = pl.program_id(1)
    qi = qi_tab[step]
    kj = kj_tab[step]
    nrep = t // LANES

    @pl.when(kj == 0)
    def _():
        m_scr[...] = jnp.full(m_scr.shape, -jnp.inf, F32)
        l_scr[...] = jnp.zeros(l_scr.shape, F32)
        acc_scr[...] = jnp.zeros(acc_scr.shape, F32)

    def update(masked):
        krow = lax.broadcasted_iota(jnp.int32, (t, LANES), 0)
        kbias = slopes_ref[h] * (krow + (kj - qi) * t).astype(F32)
        bias = jnp.concatenate([kbias] * nrep, axis=1)
        qt = qt_ref[...]
        k = k_ref[...]
        vt = vt_ref[...]
        if masked:
            causal = (lax.broadcasted_iota(jnp.int32, (t, t), 1)
                      >= lax.broadcasted_iota(jnp.int32, (t, t), 0))
        for j in range(2):
            s = _dot(k[:, j * 128:(j + 1) * 128], qt[j * 128:(j + 1) * 128, :]) + bias
            if masked:
                s = jnp.where(causal, s, -jnp.inf)
            m_prev = m_scr[j]
            m_new = jnp.maximum(m_prev, jnp.max(s, axis=0, keepdims=True))
            alpha = jnp.exp(m_prev - m_new)
            p = jnp.exp(s - m_new)
            l_scr[j] = alpha * l_scr[j] + jnp.sum(p, axis=0, keepdims=True)
            acc_scr[j] = alpha * acc_scr[j] + _dot(vt, p.astype(BF16))
            m_scr[j] = m_new

    @pl.when(kj < qi)
    def _():
        update(False)

    @pl.when(kj == qi)
    def _():
        update(True)
        lam = _lambda(lq_ref, lk_ref)
        d = acc_scr[0] / l_scr[0] - lam * (acc_scr[1] / l_scr[1])
        ms = jnp.mean(d * d, axis=0, keepdims=True)
        g = jnp.concatenate([g_ref[...]] * nrep, axis=1)
        o_ref[...] = (d * lax.rsqrt(ms + NORM_EPS) * g * (1.0 - LAM_INIT)).astype(o_ref.dtype)


def _prompt_attention(qt16, k16, vt16, lam_q, lam_k, subln_g, *, t):
    l = k16.shape[0]
    nq = l // t
    pairs = [(qi, kj) for qi in range(nq) for kj in range(qi + 1)]
    qi_tab = jnp.asarray([p[0] for p in pairs], jnp.int32)
    kj_tab = jnp.asarray([p[1] for p in pairs], jnp.int32)
    slopes = jnp.asarray(_alibi_slopes(), F32)
    g_tile = jnp.broadcast_to(subln_g.reshape(ATTN_V_DIM, 1), (ATTN_V_DIM, LANES))

    def full(shape):
        return pl.BlockSpec(shape, lambda h, s, qt, kt: tuple(0 for _ in shape))

    return pl.pallas_call(
        functools.partial(_attn_kernel, t=t),
        grid_spec=pltpu.PrefetchScalarGridSpec(
            num_scalar_prefetch=2,
            grid=(ATTN_HEADS, len(pairs)),
            in_specs=[pl.BlockSpec(memory_space=pltpu.SMEM),
                      pl.BlockSpec((256, t), lambda h, s, qt, kt: (h, qt[s])),
                      pl.BlockSpec((t, 256), lambda h, s, qt, kt: (kt[s], h)),
                      pl.BlockSpec((256, t), lambda h, s, qt, kt: (h, kt[s])),
                      full((2, 128)), full((2, 128)), full((ATTN_V_DIM, LANES))],
            out_specs=pl.BlockSpec((256, t), lambda h, s, qt, kt: (h, qt[s])),
            scratch_shapes=[pltpu.VMEM((2, 1, t), F32),
                            pltpu.VMEM((2, 1, t), F32),
                            pltpu.VMEM((2, ATTN_V_DIM, t), F32)]),
        out_shape=jax.ShapeDtypeStruct((ATTN_WIDTH, l), BF16),
        compiler_params=_params(("parallel", "arbitrary")),
        name="prompt_attn",
    )(qi_tab, kj_tab, slopes, qt16, k16, vt16, lam_q, lam_k, g_tile)


def _rows_from_lanes(row, n, width):
    return jnp.concatenate([row[:, i * width:(i + 1) * width] for i in range(n)], axis=0)


def _map_major_rows(row):
    w = ATTN_HEAD_DIM
    return jnp.concatenate([row[:, (h * 2 + j) * w:(h * 2 + j + 1) * w]
                            for j in range(2) for h in range(ATTN_HEADS)], axis=0)


def _sample_attn_kernel(pt_ref, q_ref, kn_ref, vn_ref, *rest, n_steps, pages_per_step, page, past):
    del pt_ref
    g_pages = pages_per_step
    k_refs = rest[:g_pages]
    v_refs = rest[g_pages:2 * g_pages]
    slope_ref, lq_ref, lk_ref, g_ref, o_ref, q_scr, m_scr, l_scr, acc_scr = rest[2 * g_pages:]
    p = pl.program_id(1)
    nh = ATTN_HEADS
    rows = page * nh

    @pl.when(p == 0)
    def _():
        q_scr[...] = _map_major_rows(q_ref[0]).astype(BF16)
        m_scr[...] = jnp.full(m_scr.shape, -jnp.inf, F32)
        l_scr[...] = jnp.zeros(l_scr.shape, F32)
        acc_scr[...] = jnp.zeros(acc_scr.shape, F32)

    lane = lax.broadcasted_iota(jnp.int32, (nh, rows), 1)
    own = lane % nh == lax.broadcasted_iota(jnp.int32, (nh, rows), 0)
    slope = slope_ref[...][:, :1]
    for gi in range(g_pages):
        kpos = (p * g_pages + gi) * page + lane // nh
        bias = slope * (past - kpos).astype(F32)
        pes = []
        for j in range(2):
            kj = k_refs[gi][pl.ds(j, rows, stride=2), :].astype(BF16)
            s = _dot_nt(q_scr[j * nh:(j + 1) * nh, :], kj) - bias
            s = jnp.where(own, s, -jnp.inf)
            sl = slice(j * nh, (j + 1) * nh)
            m_prev = m_scr[sl]
            m_new = jnp.maximum(m_prev, jnp.max(s, axis=-1, keepdims=True))
            alpha = jnp.exp(m_prev - m_new)
            pe = jnp.exp(s - m_new[:, :1])
            l_scr[sl] = alpha * l_scr[sl] + jnp.sum(pe, axis=-1, keepdims=True)
            acc_scr[sl] = alpha[:, :1] * acc_scr[sl]
            m_scr[sl] = m_new
            pes.append(pe.astype(BF16))
        acc_scr[...] += _dot(jnp.concatenate(pes, axis=0), v_refs[gi][...].astype(BF16))

    @pl.when(p == n_steps - 1)
    def _():
        kn = _map_major_rows(kn_ref[0]).astype(BF16).astype(F32)
        vn = _rows_from_lanes(vn_ref[0], nh, ATTN_V_DIM).astype(BF16).astype(F32)
        vn2 = jnp.concatenate([vn, vn], axis=0)
        s_self = jnp.sum(q_scr[...].astype(F32) * kn, axis=-1, keepdims=True)
        m_prev = m_scr[...]
        m_new = jnp.maximum(m_prev, s_self)
        alpha = jnp.exp(m_prev - m_new)
        p_self = jnp.exp(s_self - m_new)
        l_fin = alpha * l_scr[...] + p_self
        acc = alpha[:, :1] * acc_scr[...] + p_self[:, :1] * vn2
        n = acc / l_fin[:, :1]
        d = n[0:nh, :] - _lambda(lq_ref, lk_ref) * n[nh:2 * nh, :]
        ms = jnp.mean(d * d, axis=-1, keepdims=True)
        o_ref[0] = d * lax.rsqrt(ms + NORM_EPS) * g_ref[...] * (1.0 - LAM_INIT)


def _sample_attention(q, k_new, v_new, cache_k, cache_v, page_table, lam_q, lam_k, subln_g):
    b = q.shape[0]
    page = cache_k.shape[1]
    n_pages = page_table.shape[1]
    width = q.shape[1]
    ck = cache_k.reshape(-1, ATTN_HEAD_DIM)
    cv = cache_v.reshape(-1, ATTN_V_DIM)
    g_pages = math.gcd(n_pages, 4)
    n_steps = n_pages // g_pages
    slope_tile = jnp.asarray(np.repeat(_alibi_slopes()[:, None], LANES, axis=1), F32)
    row_spec = pl.BlockSpec((1, 1, width), lambda i, p, pt: (i, 0, 0))

    def page_spec(rows, lanes, gi):
        return pl.BlockSpec((rows, lanes), lambda i, p, pt: (pt[i * n_pages + p * g_pages + gi], 0))

    def full(shape):
        return pl.BlockSpec(shape, lambda i, p, pt: tuple(0 for _ in shape))

    out = pl.pallas_call(
        functools.partial(_sample_attn_kernel, n_steps=n_steps, pages_per_step=g_pages, page=page,
                          past=n_pages * page),
        grid_spec=pltpu.PrefetchScalarGridSpec(
            num_scalar_prefetch=1,
            grid=(b, n_steps),
            in_specs=[row_spec, row_spec, row_spec]
            + [page_spec(page * ATTN_HEADS * 2, ATTN_HEAD_DIM, gi) for gi in range(g_pages)]
            + [page_spec(page * ATTN_HEADS, ATTN_V_DIM, gi) for gi in range(g_pages)]
            + [full((ATTN_HEADS, LANES)), full((2, 128)), full((2, 128)), full((1, ATTN_V_DIM))],
            out_specs=pl.BlockSpec((1, ATTN_HEADS, ATTN_V_DIM), lambda i, p, pt: (i, 0, 0)),
            scratch_shapes=[pltpu.VMEM((2 * ATTN_HEADS, ATTN_HEAD_DIM), BF16),
                            pltpu.VMEM((2 * ATTN_HEADS, LANES), F32),
                            pltpu.VMEM((2 * ATTN_HEADS, LANES), F32),
                            pltpu.VMEM((2 * ATTN_HEADS, ATTN_V_DIM), F32)]),
        out_shape=jax.ShapeDtypeStruct((b, ATTN_HEADS, ATTN_V_DIM), F32),
        compiler_params=_params(("parallel", "arbitrary")),
        name="sample_attn",
    )(page_table.reshape(-1), q.reshape(b, 1, width), k_new.reshape(b, 1, width),
      v_new.reshape(b, 1, width), *([ck] * g_pages), *([cv] * g_pages), slope_tile, lam_q, lam_k, subln_g)
    return out.reshape(b, ATTN_WIDTH)


def _ssd_prompt_kernel(xbc_ref, z_ref, dtr_ref, cw_ref, cb_ref, dtb_ref, alog_ref, dexp_ref, ng_ref,
                       e_ref, y_ref, hfin_ref, buf, xc, ht, *, n_chunks):
    c = pl.program_id(0)
    q = SSM_CHUNK

    @pl.when(c == 0)
    def _():
        buf[0:8, :] = jnp.zeros((8, SSM_CONV_DIM), F32)
        ht[...] = jnp.zeros(ht.shape, F32)

    buf[8:8 + q, :] = xbc_ref[...]
    blk = 512
    for cbk in range(SSM_CONV_DIM // blk):
        sl = slice(cbk * blk, (cbk + 1) * blk)
        acc = cb_ref[:, sl]
        for j in range(SSM_CONV):
            acc = acc + cw_ref[j:j + 1, sl] * buf[5 + j:5 + j + q, sl]
        xc[:, sl] = _silu(acc)
    buf[0:8, :] = buf[q:q + 8, :]

    dt = _softplus(dtr_ref[...] + dtb_ref[...])
    a = dt * (-jnp.exp(alog_ref[...]))
    ri = lax.broadcasted_iota(jnp.int32, (q, q), 0)
    ci = lax.broadcasted_iota(jnp.int32, (q, q), 1)
    causal = ri >= ci
    tril = jnp.where(causal, 1.0, 0.0).astype(BF16)
    a_cum = _dot_exact_lhs(tril, a)
    a_cum_t = a_cum.T
    e16 = e_ref[...]
    ac_exp = _dot_exact_rhs(a_cum, e16)
    dt_exp = _dot_exact_rhs(dt, e16)
    ac_last = ac_exp[q - 1:q, :]
    xs = xc[:, 0:SSM_WIDTH]
    xdt = xs * dt_exp
    xw = xdt * jnp.exp(ac_last - ac_exp)
    ea = jnp.exp(ac_exp)
    cdec = jnp.exp(ac_last)
    lane_g = lax.broadcasted_iota(jnp.int32, (q, GROUP_WIDTH), 1)

    for g in range(SSM_GROUPS):
        gs = slice(g * GROUP_WIDTH, (g + 1) * GROUP_WIDTH)
        bm = xc[:, SSM_WIDTH + g * SSM_STATE:SSM_WIDTH + (g + 1) * SSM_STATE]
        cm = xc[:, SSM_WIDTH + (SSM_GROUPS + g) * SSM_STATE:SSM_WIDTH + (SSM_GROUPS + g + 1) * SSM_STATE]
        bm16 = bm.astype(BF16)
        cm16 = cm.astype(BF16)
        cbm = _dot_nt(cm16, bm16)
        xdt16 = xdt[:, gs].astype(BF16)
        yg = jnp.zeros((q, GROUP_WIDTH), F32)
        for r in range(SSM_HEADS // SSM_GROUPS):
            hd = g * (SSM_HEADS // SSM_GROUPS) + r
            seg = a_cum[:, hd:hd + 1] - a_cum_t[hd:hd + 1, :]
            dec = jnp.exp(jnp.where(causal, seg, -jnp.inf))
            yr = _dot((cbm * dec).astype(BF16), xdt16)
            yg = jnp.where(lane_g // SSM_HEAD_DIM == r, yr, yg)
        h_in = ht[:, gs]
        y_off = _dot(cm16, h_in.astype(BF16)) * ea[:, gs]
        st = _dot(bm.T.astype(BF16), xw[:, gs].astype(BF16))
        ht[:, gs] = cdec[:, gs] * h_in + st
        y = yg + y_off + dexp_ref[:, gs] * xs[:, gs]
        gated = y * _silu(z_ref[:, gs])
        ms = jnp.mean(gated * gated, axis=-1, keepdims=True)
        y_ref[:, gs] = (gated * lax.rsqrt(ms + NORM_EPS) * ng_ref[:, gs]).astype(y_ref.dtype)

    @pl.when(c == n_chunks - 1)
    def _():
        hfin_ref[...] = ht[...].T


def _head_expand_matrix():
    e = np.zeros((LANES, SSM_WIDTH), np.float32)
    for hd in range(SSM_HEADS):
        e[hd, hd * SSM_HEAD_DIM:(hd + 1) * SSM_HEAD_DIM] = 1.0
    return jnp.asarray(e, BF16)


def _pad_lanes(v):
    return jnp.pad(v.reshape(1, -1), ((0, 0), (0, LANES - v.size)))


def _ssd_prompt(xbc, z, dt_raw, conv_w, conv_b, dt_bias, a_log, d_skip, norm_g):
    l = xbc.shape[0]
    n_chunks = l // SSM_CHUNK
    q = SSM_CHUNK

    def full(shape):
        return pl.BlockSpec(shape, lambda c: tuple(0 for _ in shape))

    y16, hfin = pl.pallas_call(
        functools.partial(_ssd_prompt_kernel, n_chunks=n_chunks),
        grid=(n_chunks,),
        in_specs=[pl.BlockSpec((q, SSM_CONV_DIM), lambda c: (c, 0)),
                  pl.BlockSpec((q, SSM_WIDTH), lambda c: (c, 0)),
                  pl.BlockSpec((q, LANES), lambda c: (c, 0)),
                  full((SSM_CONV, SSM_CONV_DIM)), full((1, SSM_CONV_DIM)),
                  full((1, LANES)), full((1, LANES)), full((1, SSM_WIDTH)), full((1, SSM_WIDTH)),
                  full((LANES, SSM_WIDTH))],
        out_specs=[pl.BlockSpec((q, SSM_WIDTH), lambda c: (c, 0)),
                   full((SSM_WIDTH, SSM_STATE))],
        out_shape=[jax.ShapeDtypeStruct((l, SSM_WIDTH), BF16),
                   jax.ShapeDtypeStruct((SSM_WIDTH, SSM_STATE), F32)],
        scratch_shapes=[pltpu.VMEM((q + 8, SSM_CONV_DIM), F32),
                        pltpu.VMEM((q, SSM_CONV_DIM), F32),
                        pltpu.VMEM((SSM_STATE, SSM_WIDTH), F32)],
        compiler_params=_params(("arbitrary",)),
        name="ssd_prompt",
    )(xbc, z, dt_raw, conv_w, conv_b.reshape(1, -1), _pad_lanes(dt_bias), _pad_lanes(a_log),
      jnp.repeat(d_skip, SSM_HEAD_DIM).reshape(1, -1), norm_g.reshape(1, -1), _head_expand_matrix())
    return y16, hfin


def _ssd_sample_pre_kernel(xbc_ref, dtr_ref, sc_ref, cw_ref, cb_ref, dtb_ref, aexp_ref, e_ref,
                           xc_ref, cn_ref, xdt_t_ref, dec_t_ref):
    xbc = xbc_ref[...]
    acc = cb_ref[...] + cw_ref[SSM_CONV - 1:SSM_CONV, :] * xbc
    for j in range(SSM_CONV - 1):
        acc = acc + cw_ref[j:j + 1, :] * sc_ref[j]
    xcv = _silu(acc)
    xc_ref[...] = xcv
    cn_ref[0] = sc_ref[1]
    cn_ref[1] = sc_ref[2]
    cn_ref[2] = xbc
    dt = _softplus(dtr_ref[...] + dtb_ref[...])
    dt_exp = _dot_exact_rhs(dt, e_ref[...])
    dec_t_ref[...] = jnp.exp(dt_exp * (-jnp.exp(aexp_ref[...]))).T
    xdt_t_ref[...] = (xcv[:, 0:SSM_WIDTH] * dt_exp).T


def _ssd_sample_state_kernel(xdt_t_ref, dec_t_ref, xc_ref, s_ref, so_ref, yt_ref):
    b = pl.program_id(0)
    nb = xdt_t_ref.shape[1]
    onehot = jnp.where(lax.broadcasted_iota(jnp.int32, (nb, SSM_STATE), 0) == b, 1.0, 0.0).astype(BF16)
    lane = lax.broadcasted_iota(jnp.int32, (GROUP_WIDTH, nb), 1)

    @pl.when(b == 0)
    def _():
        yt_ref[...] = jnp.zeros(yt_ref.shape, F32)

    xc_row = xc_ref[pl.ds(b, 1), :]
    for g in range(SSM_GROUPS):
        gs = slice(g * GROUP_WIDTH, (g + 1) * GROUP_WIDTH)
        xb = _dot_exact_rhs(xdt_t_ref[gs, :], onehot)
        db = _dot_exact_rhs(dec_t_ref[gs, :], onehot)
        bm = xc_row[:, SSM_WIDTH + g * SSM_STATE:SSM_WIDTH + (g + 1) * SSM_STATE]
        cm = xc_row[:, SSM_WIDTH + (SSM_GROUPS + g) * SSM_STATE:SSM_WIDTH + (SSM_GROUPS + g + 1) * SSM_STATE]
        hn = db * s_ref[gs, :] + xb * bm
        so_ref[gs, :] = hn
        ycol = jnp.sum(hn * cm, axis=-1, keepdims=True)
        yt_ref[gs, :] = jnp.where(lane == b, ycol, yt_ref[gs, :])


def _ssd_sample_post_kernel(yt_ref, xc_ref, z_ref, dexp_ref, ng_ref, o_ref):
    yraw = yt_ref[...].T
    for g in range(SSM_GROUPS):
        gs = slice(g * GROUP_WIDTH, (g + 1) * GROUP_WIDTH)
        y = yraw[:, gs] + dexp_ref[:, gs] * xc_ref[:, gs]
        gated = y * _silu(z_ref[:, gs])
        ms = jnp.mean(gated * gated, axis=-1, keepdims=True)
        o_ref[:, gs] = gated * lax.rsqrt(ms + NORM_EPS) * ng_ref[:, gs]


def _ssd_sample(xbc, z, dt_raw, state_conv, state_ssm, conv_w, conv_b, dt_bias, a_log, d_skip, norm_g):
    nb = xbc.shape[0]
    rows = SSM_HEADS * SSM_HEAD_DIM
    sc = jnp.swapaxes(state_conv, 0, 1)
    st = state_ssm.reshape(nb * rows, SSM_STATE)
    xc, cn, xdt_t, dec_t = pl.pallas_call(
        _ssd_sample_pre_kernel,
        out_shape=[jax.ShapeDtypeStruct((nb, SSM_CONV_DIM), F32),
                   jax.ShapeDtypeStruct((SSM_CONV - 1, nb, SSM_CONV_DIM), F32),
                   jax.ShapeDtypeStruct((SSM_WIDTH, nb), F32),
                   jax.ShapeDtypeStruct((SSM_WIDTH, nb), F32)],
        compiler_params=pltpu.CompilerParams(vmem_limit_bytes=VMEM_LIMIT),
        name="ssd_sample_pre",
    )(xbc, dt_raw, sc, conv_w, conv_b.reshape(1, -1), _pad_lanes(dt_bias),
      jnp.repeat(a_log, SSM_HEAD_DIM).reshape(1, -1), _head_expand_matrix())

    def full(shape):
        return pl.BlockSpec(shape, lambda i: tuple(0 for _ in shape))

    so, yt = pl.pallas_call(
        _ssd_sample_state_kernel,
        grid=(nb,),
        in_specs=[full((SSM_WIDTH, nb)), full((SSM_WIDTH, nb)), full((nb, SSM_CONV_DIM)),
                  pl.BlockSpec((rows, SSM_STATE), lambda i: (i, 0))],
        out_specs=[pl.BlockSpec((rows, SSM_STATE), lambda i: (i, 0)), full((SSM_WIDTH, nb))],
        out_shape=[jax.ShapeDtypeStruct(st.shape, F32),
                   jax.ShapeDtypeStruct((SSM_WIDTH, nb), F32)],
        compiler_params=_params(("arbitrary",)),
        name="ssd_sample_state",
    )(xdt_t, dec_t, xc, st)
    y = pl.pallas_call(
        _ssd_sample_post_kernel,
        out_shape=jax.ShapeDtypeStruct((nb, SSM_WIDTH), F32),
        compiler_params=pltpu.CompilerParams(vmem_limit_bytes=VMEM_LIMIT),
        name="ssd_sample_post",
    )(yt, xc, z, jnp.repeat(d_skip, SSM_HEAD_DIM).reshape(1, -1), norm_g.reshape(1, -1))
    return y, jnp.swapaxes(cn, 0, 1), so.reshape(state_ssm.shape)


def _peer_query_kernel(h_ref, g_ref, wqt_ref, xnt_ref, qt_ref, xnt_scr):
    j = pl.program_id(1)

    @pl.when(j == 0)
    def _():
        x = h_ref[...]
        ms = jnp.mean(x * x, axis=-1, keepdims=True)
        xn = x * lax.rsqrt(ms + NORM_EPS) * g_ref[...]
        xnt = xn.T.astype(BF16)
        xnt_scr[...] = xnt
        xnt_ref[...] = xnt

    qt_ref[...] = _dot(wqt_ref[...], xnt_scr[...])


def _peer_query(h, g, wqt16, *, tm, tr):
    m, d = h.shape
    nq = wqt16.shape[0]
    return pl.pallas_call(
        _peer_query_kernel,
        grid=(m // tm, nq // tr),
        in_specs=[pl.BlockSpec((tm, d), lambda i, j: (i, 0)),
                  pl.BlockSpec((1, d), lambda i, j: (0, 0)),
                  pl.BlockSpec((tr, d), lambda i, j: (j, 0))],
        out_specs=[pl.BlockSpec((d, tm), lambda i, j: (0, i)),
                   pl.BlockSpec((tr, tm), lambda i, j: (j, i))],
        out_shape=[jax.ShapeDtypeStruct((d, m), BF16),
                   jax.ShapeDtypeStruct((nq, m), F32)],
        scratch_shapes=[pltpu.VMEM((d, tm), BF16)],
        compiler_params=_params(("parallel", "arbitrary")),
        name="peer_query",
    )(h, g, wqt16)


def _peer_retrieve_kernel(qt_ref, keys_ref, s1_ref, c1_ref, thr1_ref, eq1_ref, s2_ref, e2_ref, thr2_ref):
    tb = qt_ref.shape[1]
    nkeys = PEER_KEYS
    iota_n = lax.broadcasted_iota(jnp.int32, (nkeys, tb), 0)
    npair = PEER_TOPK * PEER_TOPK
    iota_p = lax.broadcasted_iota(jnp.int32, (npair, tb), 0)
    for h in range(PEER_HEADS):
        tops, scores, ranks = [], [], []
        for j in range(2):
            hj = 2 * h + j
            s = _dot(keys_ref[hj], qt_ref[hj * PEER_HALF:(hj + 1) * PEER_HALF, :].astype(BF16))
            cur = s
            rank = jnp.full((nkeys, tb), float(PEER_TOPK), F32)
            vals = []
            for k in range(PEER_TOPK):
                m = jnp.max(cur, axis=0, keepdims=True)
                first = jnp.min(jnp.where(cur == m, iota_n, nkeys), axis=0, keepdims=True)
                hit = iota_n == first
                rank = jnp.where(hit, float(k), rank)
                cur = jnp.where(hit, -jnp.inf, cur)
                vals.append(m)
            tops.append(vals)
            scores.append(s)
            ranks.append(rank)
        v2 = jnp.concatenate(tops[1], axis=0)
        cand = jnp.concatenate([tops[0][ka] + v2 for ka in range(PEER_TOPK)], axis=0)
        picked = []
        pos = None
        for k in range(PEER_TOPK):
            m = jnp.max(cand, axis=0, keepdims=True)
            pos = jnp.min(jnp.where(cand == m, iota_p, npair), axis=0, keepdims=True)
            cand = jnp.where(iota_p == pos, -jnp.inf, cand)
            picked.append(m)
        tau = picked[-1]
        tau_up = jnp.full_like(tau, jnp.inf)
        z = jnp.zeros_like(tau)
        for m in picked:
            tau_up = jnp.minimum(tau_up, jnp.where(m > tau, m, jnp.inf))
            z = z + jnp.exp(m - picked[0])
        posf = pos.astype(F32)
        ra = jnp.floor(posf / PEER_TOPK)
        rb = posf - ra * PEER_TOPK
        m1 = tops[0][0]
        m2 = tops[1][0]
        zsum = z * jnp.exp(picked[0] - (m1 + m2))
        s1_ref[h, 0] = scores[0]
        c1_ref[h, 0] = jnp.exp(scores[0] - m1) / zsum
        thr1_ref[h, 0] = jnp.where(ranks[0] < ra, tau, tau_up)
        eq1_ref[h, 0] = jnp.where(ranks[0] == ra, 1.0, 0.0)
        s2_ref[h, 0] = scores[1]
        e2_ref[h, 0] = jnp.exp(scores[1] - m2)
        thr2_ref[h, 0] = jnp.where(ranks[1] <= rb, tau, tau_up)


def _peer_retrieve(qt, keys16):
    m = qt.shape[1]
    tb = LANES
    nhj = 2 * PEER_HEADS
    spec = pl.BlockSpec((PEER_HEADS, 1, PEER_KEYS, tb), lambda i: (0, i, 0, 0))
    shape = jax.ShapeDtypeStruct((PEER_HEADS, m // tb, PEER_KEYS, tb), F32)
    return pl.pallas_call(
        _peer_retrieve_kernel,
        grid=(m // tb,),
        in_specs=[pl.BlockSpec((qt.shape[0], tb), lambda i: (0, i)),
                  pl.BlockSpec((nhj, PEER_KEYS, PEER_HALF), lambda i: (0, 0, 0))],
        out_specs=[spec] * 7,
        out_shape=[shape] * 7,
        compiler_params=_params(("parallel",)),
        name="peer_retrieve",
    )(qt, keys16)


def _gelu(x):
    return 0.5 * x * (1.0 + lax.erf(x * (2.0 ** -0.5)))


def _peer_expert_kernel(xnt_ref, s1_ref, c1_ref, thr1_ref, eq1_ref, s2_ref, e2_ref, thr2_ref,
                        u_ref, vt_ref, o_ref, act0, act1, hd, *, n_eblk, a_per_blk):
    e = pl.program_id(1)
    nk = PEER_KEYS
    tt = xnt_ref.shape[1]

    def u_dot(dst):
        dst[...] = _dot(u_ref[...], xnt_ref[...])

    def mask_gelu(src, ai):
        a = (e - 1) * a_per_blk + ai
        row = pl.ds(a, 1)
        for tl in range(tt // LANES):
            ls = slice(tl * LANES, (tl + 1) * LANES)
            w = jnp.zeros((nk, LANES), F32)
            for h in range(PEER_HEADS):
                thr = jnp.where(eq1_ref[h, tl, row, :] > 0.5, thr2_ref[h, tl], thr1_ref[h, tl, row, :])
                pair = s2_ref[h, tl] + s1_ref[h, tl, row, :]
                w = w + jnp.where(pair >= thr, e2_ref[h, tl] * c1_ref[h, tl, row, :], 0.0)
            hd[ai * nk:(ai + 1) * nk, ls] = (_gelu(src[ai * nk:(ai + 1) * nk, ls]) * w).astype(BF16)

    def skewed(dst, src):
        u_dot(dst)
        for ai in range(a_per_blk):
            mask_gelu(src, ai)
        o_ref[...] += _dot(vt_ref[...], hd[...])

    @pl.when(e == 0)
    def _():
        o_ref[...] = jnp.zeros(o_ref.shape, F32)
        u_dot(act0)

    @pl.when((e > 0) & (e < n_eblk) & (e % 2 == 1))
    def _():
        skewed(act1, act0)

    @pl.when((e > 0) & (e < n_eblk) & (e % 2 == 0))
    def _():
        skewed(act0, act1)

    @pl.when(e == n_eblk)
    def _():
        src = act1 if (n_eblk - 1) % 2 == 1 else act0
        for ai in range(a_per_blk):
            mask_gelu(src, ai)
        o_ref[...] += _dot(vt_ref[...], hd[...])


def _peer_experts(xnt, tiles, u16, vt16, *, tt, eb):
    d, m = xnt.shape
    n_exp = u16.shape[0]
    n_eblk = n_exp // eb
    once = pl.Buffered(1)
    tile_spec = pl.BlockSpec((PEER_HEADS, tt // LANES, PEER_KEYS, LANES), lambda i, e: (0, i, 0, 0),
                             pipeline_mode=once)
    return pl.pallas_call(
        functools.partial(_peer_expert_kernel, n_eblk=n_eblk, a_per_blk=eb // PEER_KEYS),
        grid=(m // tt, n_eblk + 1),
        in_specs=[pl.BlockSpec((d, tt), lambda i, e: (0, i), pipeline_mode=once)]
        + [tile_spec] * 7
        + [pl.BlockSpec((eb, d), lambda i, e: (jnp.minimum(e, n_eblk - 1), 0)),
           pl.BlockSpec((d, eb), lambda i, e: (0, jnp.maximum(e - 1, 0)))],
        out_specs=pl.BlockSpec((d, tt), lambda i, e: (0, i), pipeline_mode=once),
        out_shape=jax.ShapeDtypeStruct((d, m), F32),
        scratch_shapes=[pltpu.VMEM((eb, tt), F32), pltpu.VMEM((eb, tt), F32), pltpu.VMEM((eb, tt), BF16)],
        compiler_params=_params(("parallel", "arbitrary")),
        name="peer_experts",
    )(xnt, *tiles, u16, vt16)


def _final_kernel(h_ref, ft_ref, g_ref, o_ref):
    x = h_ref[...] + ft_ref[...].T
    ms = jnp.mean(x * x, axis=-1, keepdims=True)
    o_ref[...] = x * lax.rsqrt(ms + NORM_EPS) * g_ref[...]


def _final_norm(h, ffn_t, g, *, tm):
    m, d = h.shape
    row = pl.BlockSpec((tm, d), lambda i: (i, 0))
    return pl.pallas_call(
        _final_kernel,
        grid=(m // tm,),
        in_specs=[row, pl.BlockSpec((d, tm), lambda i: (0, i)), pl.BlockSpec((1, d), lambda i: (0, 0))],
        out_specs=row,
        out_shape=jax.ShapeDtypeStruct((m, d), F32),
        compiler_params=_params(("parallel",)),
        name="final_norm",
    )(h, ffn_t, g)


def _tile(m, pref):
    return pref if m % pref == 0 else m


def _in_proj(x, g_mix, w16, wdt16, *, transposed):
    m = x.shape[0]
    tm = _tile(m, 512)
    qw = 2 * ATTN_HEADS * ATTN_HEAD_DIM
    t16 = "bf16T" if transposed else "bf16"
    segs = [(0, qw, ATTN_SCALE, (t16,)), (qw, qw, 1.0, ("f32", "bf16")), (2 * qw, ATTN_WIDTH, 1.0, ("f32", t16)),
            (2 * qw + ATTN_WIDTH, SSM_WIDTH, 1.0, ("f32",)),
            (2 * qw + ATTN_WIDTH + SSM_WIDTH, SSM_CONV_DIM, 1.0, ("f32",))]
    q16, k32, k16, v32, v16, z32, xbc32, xn16 = _norm_mm(x, g_mix, w16, segs, tm=tm, tn=512, emit_xn=True)
    dt_raw = _mm([xn16], [wdt16], None, tm=tm, tn=LANES)
    return q16, k32, k16, v32, v16, z32, xbc32, dt_raw


def _peer_and_final(h, g_ffn, wqt16, keys16, u16, vt16, g_final):
    m = h.shape[0]
    xnt, qt = _peer_query(h, g_ffn, wqt16, tm=_tile(m, 256), tr=512)
    tiles = _peer_retrieve(qt, keys16)
    ffn_t = _peer_experts(xnt, tiles, u16, vt16, tt=_tile(m, 512), eb=512)
    return _final_norm(h, ffn_t, g_final, tm=_tile(m, 256))


def kernel(x_prompt, x_sample, cache_k, cache_v, state_conv, state_ssm, page_table, g_mix, w_in, lambda_q, lambda_k, subln_g, conv_w, conv_b, dt_bias, a_log, d_skip, ssm_norm_g, w_out, g_ffn, peer_wq, peer_keys, peer_u, peer_v, g_final):
    assert w_in.shape[0] == 1, "single trunk layer"
    bsz, seq, dm = x_prompt.shape
    nb = x_sample.shape[0]
    assert bsz == 1 and x_sample.shape[1] == 1
    main_cols = w_in.shape[2] - SSM_HEADS
    w16 = w_in[0].astype(BF16)
    wdt16 = jnp.pad(w_in[0][:, main_cols:], ((0, 0), (0, LANES - SSM_HEADS))).astype(BF16)
    wo16 = w_out[0].astype(BF16)
    wqt16 = peer_wq[0].T.astype(BF16)
    keys16 = peer_keys[0].reshape(2 * PEER_HEADS, PEER_KEYS, PEER_HALF).astype(BF16)
    u16 = peer_u[0].astype(BF16)
    vt16 = peer_v[0].T.astype(BF16)
    gm = g_mix[0].reshape(1, -1)
    gf = g_ffn[0].reshape(1, -1)
    gfin = g_final.reshape(1, -1)
    sg = subln_g[0].reshape(1, -1)
    lq, lk = lambda_q[0], lambda_k[0]

    xp = x_prompt.reshape(seq, dm)
    qt16, k32, k16, v32, vtok16, z32, xbc32, dtr = _in_proj(xp, gm, w16, wdt16, transposed=True)
    attn16 = _prompt_attention(qt16, k16, vtok16, lq, lk, sg, t=_tile(seq, 512)).T
    y16, hfin = _ssd_prompt(xbc32, z32, dtr, conv_w[0], conv_b[0], dt_bias[0], a_log[0], d_skip[0],
                            ssm_norm_g[0])
    hp = _mm([attn16, y16], [wo16[:ATTN_WIDTH], wo16[ATTN_WIDTH:]], xp, tm=_tile(seq, 512), tn=512)
    y_prompt = _peer_and_final(hp, gf, wqt16, keys16, u16, vt16, gfin)

    xs = x_sample.reshape(nb, dm)
    sq16, sk32, _, sv32, _, sz32, sxbc32, sdtr = _in_proj(xs, gm, w16, wdt16, transposed=False)
    sattn = _sample_attention(sq16.astype(F32), sk32, sv32, cache_k[0], cache_v[0], page_table, lq, lk, sg)
    sy, conv_s, ssm_s = _ssd_sample(sxbc32, sz32, sdtr, state_conv[0], state_ssm[0], conv_w[0], conv_b[0],
                                    dt_bias[0], a_log[0], d_skip[0], ssm_norm_g[0])
    hs = _mm([sattn.astype(BF16), sy.astype(BF16)], [wo16[:ATTN_WIDTH], wo16[ATTN_WIDTH:]], xs,
             tm=nb, tn=512)
    y_sample = _peer_and_final(hs, gf, wqt16, keys16, u16, vt16, gfin)

    return (y_prompt.reshape(bsz, seq, dm),
            y_sample.reshape(nb, 1, dm),
            k32.reshape(1, bsz, seq, ATTN_HEADS, 2, ATTN_HEAD_DIM),
            v32.reshape(1, bsz, seq, ATTN_HEADS, ATTN_V_DIM),
            xbc32[seq - (SSM_CONV - 1):].reshape(1, bsz, SSM_CONV - 1, SSM_CONV_DIM),
            hfin.reshape(1, bsz, SSM_HEADS, SSM_HEAD_DIM, SSM_STATE),
            sk32.reshape(1, nb, 1, ATTN_HEADS, 2, ATTN_HEAD_DIM),
            sv32.reshape(1, nb, 1, ATTN_HEADS, ATTN_V_DIM),
            conv_s.reshape(1, nb, SSM_CONV - 1, SSM_CONV_DIM),
            ssm_s.reshape(1, nb, SSM_HEADS, SSM_HEAD_DIM, SSM_STATE))
```

```python
import functools
import math

import jax
import jax.numpy as jnp
import numpy as np
from jax import lax
from jax.experimental import pallas as pl
from jax.experimental.pallas import tpu as pltpu

F32 = jnp.float32
BF16 = jnp.bfloat16

ATTN_HEADS = 8
ATTN_HEAD_DIM = 128
ATTN_V_DIM = 256
ATTN_WIDTH = ATTN_HEADS * ATTN_V_DIM
SSM_WIDTH = 2048
SSM_HEAD_DIM = 64
SSM_HEADS = 32
SSM_GROUPS = 8
SSM_STATE = 128
SSM_CONV = 4
SSM_CHUNK = 128
SSM_CONV_DIM = SSM_WIDTH + 2 * SSM_GROUPS * SSM_STATE
GROUP_WIDTH = SSM_WIDTH // SSM_GROUPS
PEER_HEADS = 8
PEER_KEYS = 128
PEER_HALF = 128
PEER_TOPK = 16
NORM_EPS = 1e-6
LAM_INIT = 0.8 - 0.6 * math.exp(-0.3 * 0)
ATTN_SCALE = ATTN_HEAD_DIM ** -0.5
LANES = 128
VMEM_LIMIT = 56 * 1024 * 1024


def _params(sem):
    return pltpu.CompilerParams(dimension_semantics=sem, vmem_limit_bytes=VMEM_LIMIT)


def _split3(x):
    x1 = x.astype(BF16)
    r1 = x - x1.astype(F32)
    x2 = r1.astype(BF16)
    x3 = (r1 - x2.astype(F32)).astype(BF16)
    return x1, x2, x3


def _dot(a, b):
    return jnp.dot(a, b, preferred_element_type=F32)


def _dot_nt(a, b):
    return lax.dot_general(a, b, (((1,), (1,)), ((), ())), preferred_element_type=F32)


def _dot_exact_rhs(x, w16):
    x1, x2, x3 = _split3(x)
    return _dot(x1, w16) + _dot(x2, w16) + _dot(x3, w16)


def _dot_exact_lhs(w16, x):
    x1, x2, x3 = _split3(x)
    return _dot(w16, x1) + _dot(w16, x2) + _dot(w16, x3)


def _silu(x):
    return x / (1.0 + jnp.exp(-x))


def _softplus(x):
    return jnp.maximum(x, 0.0) + jnp.log1p(jnp.exp(-jnp.abs(x)))


def _lambda(lq_ref, lk_ref):
    e = jnp.exp(jnp.sum(lq_ref[...] * lk_ref[...], axis=-1, keepdims=True))
    return e[0:1, :] - e[1:2, :] + LAM_INIT


def _alibi_slopes():
    return 2.0 ** (-8.0 * np.arange(1, ATTN_HEADS + 1) / ATTN_HEADS)


def _norm_mm_kernel(x_ref, g_ref, w_ref, *rest, segs, emit_xn):
    outs = list(rest[:-1])
    xn_scr = rest[-1]
    j = pl.program_id(1)

    @pl.when(j == 0)
    def _():
        x = x_ref[...]
        ms = jnp.mean(x * x, axis=-1, keepdims=True)
        xn = (x * lax.rsqrt(ms + NORM_EPS) * g_ref[...]).astype(BF16)
        xn_scr[...] = xn
        if emit_xn:
            outs[-1][...] = xn

    acc = _dot(xn_scr[...], w_ref[...])
    k = 0
    for (start, nblk, scale, kinds) in segs:
        refs = [(outs[k + i], kind) for i, kind in enumerate(kinds)]
        k += len(kinds)

        @pl.when((j >= start) & (j < start + nblk))
        def _(refs=refs, scale=scale):
            val = acc if scale == 1.0 else acc * scale
            for r, kind in refs:
                r[...] = (val.T if kind.endswith("T") else val).astype(r.dtype)


def _norm_mm(x, g, w16, segs, *, tm, tn, emit_xn):
    m, kdim = x.shape
    nblocks = sum(wd for _, wd, _, _ in segs) // tn
    bsegs = [(s // tn, wd // tn, scale, kinds) for s, wd, scale, kinds in segs]
    out_shapes, out_specs = [], []
    for (sb, nb, _, kinds), (_, wd, _, _) in zip(bsegs, segs):
        def imap(i, j, sb=sb, nb=nb):
            return (i, jnp.clip(j - sb, 0, nb - 1))

        def imap_t(i, j, sb=sb, nb=nb):
            return (jnp.clip(j - sb, 0, nb - 1), i)
        for kind in kinds:
            dt = F32 if kind == "f32" else BF16
            if kind.endswith("T"):
                out_shapes.append(jax.ShapeDtypeStruct((wd, m), dt))
                out_specs.append(pl.BlockSpec((tn, tm), imap_t))
            else:
                out_shapes.append(jax.ShapeDtypeStruct((m, wd), dt))
                out_specs.append(pl.BlockSpec((tm, tn), imap))
    if emit_xn:
        out_shapes.append(jax.ShapeDtypeStruct((m, kdim), BF16))
        out_specs.append(pl.BlockSpec((tm, kdim), lambda i, j: (i, 0)))
    return pl.pallas_call(
        functools.partial(_norm_mm_kernel, segs=bsegs, emit_xn=emit_xn),
        grid=(m // tm, nblocks),
        in_specs=[pl.BlockSpec((tm, kdim), lambda i, j: (i, 0), pipeline_mode=pl.Buffered(1)),
                  pl.BlockSpec((1, kdim), lambda i, j: (0, 0)),
                  pl.BlockSpec((kdim, tn), lambda i, j: (0, j))],
        out_specs=out_specs,
        out_shape=out_shapes,
        scratch_shapes=[pltpu.VMEM((tm, kdim), BF16)],
        compiler_params=_params(("parallel", "arbitrary")),
        name="norm_mm",
    )(x, g, w16)


def _mm_kernel(*refs, n_pairs, has_res):
    o_ref = refs[-1]
    acc = _dot(refs[0][...], refs[n_pairs][...])
    for i in range(1, n_pairs):
        acc = acc + _dot(refs[i][...], refs[n_pairs + i][...])
    if has_res:
        acc = acc + refs[2 * n_pairs][...]
    o_ref[...] = acc


def _mm(a_list, w_list, res, *, tm, tn):
    m = a_list[0].shape[0]
    n = w_list[0].shape[1]
    in_specs = [pl.BlockSpec((tm, a.shape[1]), lambda i, j: (i, 0)) for a in a_list]
    in_specs += [pl.BlockSpec((w.shape[0], tn), lambda i, j: (0, j)) for w in w_list]
    args = list(a_list) + list(w_list)
    if res is not None:
        in_specs.append(pl.BlockSpec((tm, tn), lambda i, j: (i, j)))
        args.append(res)
    return pl.pallas_call(
        functools.partial(_mm_kernel, n_pairs=len(a_list), has_res=res is not None),
        grid=(m // tm, n // tn),
        in_specs=in_specs,
        out_specs=pl.BlockSpec((tm, tn), lambda i, j: (i, j)),
        out_shape=jax.ShapeDtypeStruct((m, n), F32),
        compiler_params=_params(("parallel", "parallel")),
        name="mm",
    )(*args)


def _attn_kernel(qi_tab, kj_tab, slopes_ref, qt_ref, k_ref, vt_ref, lq_ref, lk_ref, g_ref, o_ref,
                 m_scr, l_scr, acc_scr, *, t):
    h = pl.program_id(0)
    step = pl.program_id(1)
    qi = qi_tab[step]
    kj = kj_tab[step]
    nrep = t // LANES

    @pl.when(kj == 0)
    def _():
        m_scr[...] = jnp.full(m_scr.shape, -jnp.inf, F32)
        l_scr[...] = jnp.zeros(l_scr.shape, F32)
        acc_scr[...] = jnp.zeros(acc_scr.shape, F32)

    def update(masked):
        krow = lax.broadcasted_iota(jnp.int32, (t, LANES), 0)
        kbias = slopes_ref[h] * (krow + (kj - qi) * t).astype(F32)
        bias = jnp.concatenate([kbias] * nrep, axis=1)
        qt = qt_ref[...]
        k = k_ref[...]
        vt = vt_ref[...]
        if masked:
            causal = (lax.broadcasted_iota(jnp.int32, (t, t), 1)
                      >= lax.broadcasted_iota(jnp.int32, (t, t), 0))
        for j in range(2):
            s = _dot(k[:, j * 128:(j + 1) * 128], qt[j * 128:(j + 1) * 128, :]) + bias
            if masked:
                s = jnp.where(causal, s, -jnp.inf)
            m_prev = m_scr[j]
            m_new = jnp.maximum(m_prev, jnp.max(s, axis=0, keepdims=True))
            alpha = jnp.exp(m_prev - m_new)
            p = jnp.exp(s - m_new)
            l_scr[j] = alpha * l_scr[j] + jnp.sum(p, axis=0, keepdims=True)
            acc_scr[j] = alpha * acc_scr[j] + _dot(vt, p.astype(BF16))
            m_scr[j] = m_new

    @pl.when(kj < qi)
    def _():
        update(False)

    @pl.when(kj == qi)
    def _():
        update(True)
        lam = _lambda(lq_ref, lk_ref)
        d = acc_scr[0] / l_scr[0] - lam * (acc_scr[1] / l_scr[1])
        ms = jnp.mean(d * d, axis=0, keepdims=True)
        g = jnp.concatenate([g_ref[...]] * nrep, axis=1)
        o_ref[...] = (d * lax.rsqrt(ms + NORM_EPS) * g * (1.0 - LAM_INIT)).astype(o_ref.dtype)


def _prompt_attention(qt16, k16, vt16, lam_q, lam_k, subln_g, *, t):
    l = k16.shape[0]
    nq = l // t
    pairs = [(qi, kj) for qi in range(nq) for kj in range(qi + 1)]
    qi_tab = jnp.asarray([p[0] for p in pairs], jnp.int32)
    kj_tab = jnp.asarray([p[1] for p in pairs], jnp.int32)
    slopes = jnp.asarray(_alibi_slopes(), F32)
    g_tile = jnp.broadcast_to(subln_g.reshape(ATTN_V_DIM, 1), (ATTN_V_DIM, LANES))

    def full(shape):
        return pl.BlockSpec(shape, lambda h, s, qt, kt: tuple(0 for _ in shape))

    return pl.pallas_call(
        functools.partial(_attn_kernel, t=t),
        grid_spec=pltpu.PrefetchScalarGridSpec(
            num_scalar_prefetch=2,
            grid=(ATTN_HEADS, len(pairs)),
            in_specs=[pl.BlockSpec(memory_space=pltpu.SMEM),
                      pl.BlockSpec((256, t), lambda h, s, qt, kt: (h, qt[s])),
                      pl.BlockSpec((t, 256), lambda h, s, qt, kt: (kt[s], h)),
                      pl.BlockSpec((256, t), lambda h, s, qt, kt: (h, kt[s])),
                      full((2, 128)), full((2, 128)), full((ATTN_V_DIM, LANES))],
            out_specs=pl.BlockSpec((256, t), lambda h, s, qt, kt: (h, qt[s])),
            scratch_shapes=[pltpu.VMEM((2, 1, t), F32),
                            pltpu.VMEM((2, 1, t), F32),
                            pltpu.VMEM((2, ATTN_V_DIM, t), F32)]),
        out_shape=jax.ShapeDtypeStruct((ATTN_WIDTH, l), BF16),
        compiler_params=_params(("parallel", "arbitrary")),
        name="prompt_attn",
    )(qi_tab, kj_tab, slopes, qt16, k16, vt16, lam_q, lam_k, g_tile)


def _rows_from_lanes(row, n, width):
    return jnp.concatenate([row[:, i * width:(i + 1) * width] for i in range(n)], axis=0)


def _map_major_rows(row):
    w = ATTN_HEAD_DIM
    return jnp.concatenate([row[:, (h * 2 + j) * w:(h * 2 + j + 1) * w]
                            for j in range(2) for h in range(ATTN_HEADS)], axis=0)


def _sample_attn_kernel(pt_ref, q_ref, kn_ref, vn_ref, *rest, n_steps, pages_per_step, page, past):
    del pt_ref
    g_pages = pages_per_step
    k_refs = rest[:g_pages]
    v_refs = rest[g_pages:2 * g_pages]
    slope_ref, lq_ref, lk_ref, g_ref, o_ref, q_scr, m_scr, l_scr, acc_scr = rest[2 * g_pages:]
    p = pl.program_id(1)
    nh = ATTN_HEADS
    rows = page * nh

    @pl.when(p == 0)
    def _():
        q_scr[...] = _map_major_rows(q_ref[0]).astype(BF16)
        m_scr[...] = jnp.full(m_scr.shape, -jnp.inf, F32)
        l_scr[...] = jnp.zeros(l_scr.shape, F32)
        acc_scr[...] = jnp.zeros(acc_scr.shape, F32)

    lane = lax.broadcasted_iota(jnp.int32, (nh, rows), 1)
    own = lane % nh == lax.broadcasted_iota(jnp.int32, (nh, rows), 0)
    slope = slope_ref[...][:, :1]
    scores = []
    for gi in range(g_pages):
        kpos = (p * g_pages + gi) * page + lane // nh
        bias = slope * (past - kpos).astype(F32)
        sj = []
        for j in range(2):
            kj = k_refs[gi][pl.ds(j, rows, stride=2), :].astype(BF16)
            s = _dot_nt(q_scr[j * nh:(j + 1) * nh, :], kj) - bias
            sj.append(jnp.where(own, s, -jnp.inf))
        scores.append(jnp.concatenate(sj, axis=0))
    m_prev = m_scr[...]
    m_new = m_prev
    for s in scores:
        m_new = jnp.maximum(m_new, jnp.max(s, axis=-1, keepdims=True))
    alpha = jnp.exp(m_prev - m_new)
    l_new = alpha * l_scr[...]
    acc = alpha[:, :1] * acc_scr[...]
    for gi, s in enumerate(scores):
        pe = jnp.exp(s - m_new[:, :1])
        l_new = l_new + jnp.sum(pe, axis=-1, keepdims=True)
        acc = acc + _dot(pe.astype(BF16), v_refs[gi][...].astype(BF16))
    m_scr[...] = m_new
    l_scr[...] = l_new
    acc_scr[...] = acc

    @pl.when(p == n_steps - 1)
    def _():
        kn = _map_major_rows(kn_ref[0]).astype(BF16).astype(F32)
        vn = _rows_from_lanes(vn_ref[0], nh, ATTN_V_DIM).astype(BF16).astype(F32)
        vn2 = jnp.concatenate([vn, vn], axis=0)
        s_self = jnp.sum(q_scr[...].astype(F32) * kn, axis=-1, keepdims=True)
        m_prev = m_scr[...]
        m_new = jnp.maximum(m_prev, s_self)
        alpha = jnp.exp(m_prev - m_new)
        p_self = jnp.exp(s_self - m_new)
        l_fin = alpha * l_scr[...] + p_self
        acc = alpha[:, :1] * acc_scr[...] + p_self[:, :1] * vn2
        n = acc / l_fin[:, :1]
        d = n[0:nh, :] - _lambda(lq_ref, lk_ref) * n[nh:2 * nh, :]
        ms = jnp.mean(d * d, axis=-1, keepdims=True)
        o_ref[0] = d * lax.rsqrt(ms + NORM_EPS) * g_ref[...] * (1.0 - LAM_INIT)


def _sample_attention(q, k_new, v_new, cache_k, cache_v, page_table, lam_q, lam_k, subln_g):
    b = q.shape[0]
    page = cache_k.shape[1]
    n_pages = page_table.shape[1]
    width = q.shape[1]
    ck = cache_k.reshape(-1, ATTN_HEAD_DIM)
    cv = cache_v.reshape(-1, ATTN_V_DIM)
    g_pages = math.gcd(n_pages, 8)
    n_steps = n_pages // g_pages
    slope_tile = jnp.asarray(np.repeat(_alibi_slopes()[:, None], LANES, axis=1), F32)
    row_spec = pl.BlockSpec((1, 1, width), lambda i, p, pt: (i, 0, 0))

    def page_spec(rows, lanes, gi):
        return pl.BlockSpec((rows, lanes), lambda i, p, pt: (pt[i * n_pages + p * g_pages + gi], 0))

    def full(shape):
        return pl.BlockSpec(shape, lambda i, p, pt: tuple(0 for _ in shape))

    out = pl.pallas_call(
        functools.partial(_sample_attn_kernel, n_steps=n_steps, pages_per_step=g_pages, page=page,
                          past=n_pages * page),
        grid_spec=pltpu.PrefetchScalarGridSpec(
            num_scalar_prefetch=1,
            grid=(b, n_steps),
            in_specs=[row_spec, row_spec, row_spec]
            + [page_spec(page * ATTN_HEADS * 2, ATTN_HEAD_DIM, gi) for gi in range(g_pages)]
            + [page_spec(page * ATTN_HEADS, ATTN_V_DIM, gi) for gi in range(g_pages)]
            + [full((ATTN_HEADS, LANES)), full((2, 128)), full((2, 128)), full((1, ATTN_V_DIM))],
            out_specs=pl.BlockSpec((1, ATTN_HEADS, ATTN_V_DIM), lambda i, p, pt: (i, 0, 0)),
            scratch_shapes=[pltpu.VMEM((2 * ATTN_HEADS, ATTN_HEAD_DIM), BF16),
                            pltpu.VMEM((2 * ATTN_HEADS, LANES), F32),
                            pltpu.VMEM((2 * ATTN_HEADS, LANES), F32),
                            pltpu.VMEM((2 * ATTN_HEADS, ATTN_V_DIM), F32)]),
        out_shape=jax.ShapeDtypeStruct((b, ATTN_HEADS, ATTN_V_DIM), F32),
        compiler_params=_params(("parallel", "arbitrary")),
        name="sample_attn",
    )(page_table.reshape(-1), q.reshape(b, 1, width), k_new.reshape(b, 1, width),
      v_new.reshape(b, 1, width), *([ck] * g_pages), *([cv] * g_pages), slope_tile, lam_q, lam_k, subln_g)
    return out.reshape(b, ATTN_WIDTH)


def _ssd_prompt_kernel(xbc_ref, z_ref, dtr_ref, cw_ref, cb_ref, dtb_ref, alog_ref, dexp_ref, ng_ref,
                       e_ref, y_ref, hfin_ref, buf, xc, ht, *, n_chunks):
    c = pl.program_id(0)
    q = SSM_CHUNK

    @pl.when(c == 0)
    def _():
        buf[0:8, :] = jnp.zeros((8, SSM_CONV_DIM), F32)
        ht[...] = jnp.zeros(ht.shape, F32)

    buf[8:8 + q, :] = xbc_ref[...]
    blk = 512
    for cbk in range(SSM_CONV_DIM // blk):
        sl = slice(cbk * blk, (cbk + 1) * blk)
        acc = cb_ref[:, sl]
        for j in range(SSM_CONV):
            acc = acc + cw_ref[j:j + 1, sl] * buf[5 + j:5 + j + q, sl]
        xc[:, sl] = _silu(acc)
    buf[0:8, :] = buf[q:q + 8, :]

    dt = _softplus(dtr_ref[...] + dtb_ref[...])
    a = dt * (-jnp.exp(alog_ref[...]))
    ri = lax.broadcasted_iota(jnp.int32, (q, q), 0)
    ci = lax.broadcasted_iota(jnp.int32, (q, q), 1)
    causal = ri >= ci
    tril = jnp.where(causal, 1.0, 0.0).astype(BF16)
    a_cum = _dot_exact_lhs(tril, a)
    a_cum_t = a_cum.T
    e16 = e_ref[...]
    ac_exp = _dot_exact_rhs(a_cum, e16)
    dt_exp = _dot_exact_rhs(dt, e16)
    ac_last = ac_exp[q - 1:q, :]
    xs = xc[:, 0:SSM_WIDTH]
    xdt = xs * dt_exp
    xw = xdt * jnp.exp(ac_last - ac_exp)
    ea = jnp.exp(ac_exp)
    cdec = jnp.exp(ac_last)
    lane_g = lax.broadcasted_iota(jnp.int32, (q, GROUP_WIDTH), 1)

    for g in range(SSM_GROUPS):
        gs = slice(g * GROUP_WIDTH, (g + 1) * GROUP_WIDTH)
        bm = xc[:, SSM_WIDTH + g * SSM_STATE:SSM_WIDTH + (g + 1) * SSM_STATE]
        cm = xc[:, SSM_WIDTH + (SSM_GROUPS + g) * SSM_STATE:SSM_WIDTH + (SSM_GROUPS + g + 1) * SSM_STATE]
        bm16 = bm.astype(BF16)
        cm16 = cm.astype(BF16)
        cbm = _dot_nt(cm16, bm16)
        xdt16 = xdt[:, gs].astype(BF16)
        yg = jnp.zeros((q, GROUP_WIDTH), F32)
        for r in range(SSM_HEADS // SSM_GROUPS):
            hd = g * (SSM_HEADS // SSM_GROUPS) + r
            seg = a_cum[:, hd:hd + 1] - a_cum_t[hd:hd + 1, :]
            dec = jnp.exp(jnp.where(causal, seg, -jnp.inf))
            yr = _dot((cbm * dec).astype(BF16), xdt16)
            yg = jnp.where(lane_g // SSM_HEAD_DIM == r, yr, yg)
        h_in = ht[:, gs]
        y_off = _dot(cm16, h_in.astype(BF16)) * ea[:, gs]
        st = _dot(bm.T.astype(BF16), xw[:, gs].astype(BF16))
        ht[:, gs] = cdec[:, gs] * h_in + st
        y = yg + y_off + dexp_ref[:, gs] * xs[:, gs]
        gated = y * _silu(z_ref[:, gs])
        ms = jnp.mean(gated * gated, axis=-1, keepdims=True)
        y_ref[:, gs] = (gated * lax.rsqrt(ms + NORM_EPS) * ng_ref[:, gs]).astype(y_ref.dtype)

    @pl.when(c == n_chunks - 1)
    def _():
        hfin_ref[...] = ht[...].T


def _head_expand_matrix():
    e = np.zeros((LANES, SSM_WIDTH), np.float32)
    for hd in range(SSM_HEADS):
        e[hd, hd * SSM_HEAD_DIM:(hd + 1) * SSM_HEAD_DIM] = 1.0
    return jnp.asarray(e, BF16)


def _pad_lanes(v):
    return jnp.pad(v.reshape(1, -1), ((0, 0), (0, LANES - v.size)))


def _ssd_prompt(xbc, z, dt_raw, conv_w, conv_b, dt_bias, a_log, d_skip, norm_g):
    l = xbc.shape[0]
    n_chunks = l // SSM_CHUNK
    q = SSM_CHUNK

    def full(shape):
        return pl.BlockSpec(shape, lambda c: tuple(0 for _ in shape))

    y16, hfin = pl.pallas_call(
        functools.partial(_ssd_prompt_kernel, n_chunks=n_chunks),
        grid=(n_chunks,),
        in_specs=[pl.BlockSpec((q, SSM_CONV_DIM), lambda c: (c, 0)),
                  pl.BlockSpec((q, SSM_WIDTH), lambda c: (c, 0)),
                  pl.BlockSpec((q, LANES), lambda c: (c, 0)),
                  full((SSM_CONV, SSM_CONV_DIM)), full((1, SSM_CONV_DIM)),
                  full((1, LANES)), full((1, LANES)), full((1, SSM_WIDTH)), full((1, SSM_WIDTH)),
                  full((LANES, SSM_WIDTH))],
        out_specs=[pl.BlockSpec((q, SSM_WIDTH), lambda c: (c, 0)),
                   full((SSM_WIDTH, SSM_STATE))],
        out_shape=[jax.ShapeDtypeStruct((l, SSM_WIDTH), BF16),
                   jax.ShapeDtypeStruct((SSM_WIDTH, SSM_STATE), F32)],
        scratch_shapes=[pltpu.VMEM((q + 8, SSM_CONV_DIM), F32),
                        pltpu.VMEM((q, SSM_CONV_DIM), F32),
                        pltpu.VMEM((SSM_STATE, SSM_WIDTH), F32)],
        compiler_params=_params(("arbitrary",)),
        name="ssd_prompt",
    )(xbc, z, dt_raw, conv_w, conv_b.reshape(1, -1), _pad_lanes(dt_bias), _pad_lanes(a_log),
      jnp.repeat(d_skip, SSM_HEAD_DIM).reshape(1, -1), norm_g.reshape(1, -1), _head_expand_matrix())
    return y16, hfin


def _ssd_sample_pre_kernel(xbc_ref, dtr_ref, sc_ref, cw_ref, cb_ref, dtb_ref, aexp_ref, e_ref,
                           xc_ref, cn_ref, xdt_t_ref, dec_t_ref):
    xbc = xbc_ref[...]
    acc = cb_ref[...] + cw_ref[SSM_CONV - 1:SSM_CONV, :] * xbc
    for j in range(SSM_CONV - 1):
        acc = acc + cw_ref[j:j + 1, :] * sc_ref[j]
    xcv = _silu(acc)
    xc_ref[...] = xcv
    cn_ref[0] = sc_ref[1]
    cn_ref[1] = sc_ref[2]
    cn_ref[2] = xbc
    dt = _softplus(dtr_ref[...] + dtb_ref[...])
    dt_exp = _dot_exact_rhs(dt, e_ref[...])
    dec_t_ref[...] = jnp.exp(dt_exp * (-jnp.exp(aexp_ref[...]))).T
    xdt_t_ref[...] = (xcv[:, 0:SSM_WIDTH] * dt_exp).T


def _ssd_sample_state_kernel(xdt_t_ref, dec_t_ref, xc_ref, s_ref, so_ref, yt_ref):
    b = pl.program_id(0)
    nb = xdt_t_ref.shape[1]
    onehot = jnp.where(lax.broadcasted_iota(jnp.int32, (nb, SSM_STATE), 0) == b, 1.0, 0.0).astype(BF16)
    lane = lax.broadcasted_iota(jnp.int32, (GROUP_WIDTH, nb), 1)

    @pl.when(b == 0)
    def _():
        yt_ref[...] = jnp.zeros(yt_ref.shape, F32)

    xc_row = xc_ref[pl.ds(b, 1), :]
    for g in range(SSM_GROUPS):
        gs = slice(g * GROUP_WIDTH, (g + 1) * GROUP_WIDTH)
        xb = _dot_exact_rhs(xdt_t_ref[gs, :], onehot)
        db = _dot_exact_rhs(dec_t_ref[gs, :], onehot)
        bm = xc_row[:, SSM_WIDTH + g * SSM_STATE:SSM_WIDTH + (g + 1) * SSM_STATE]
        cm = xc_row[:, SSM_WIDTH + (SSM_GROUPS + g) * SSM_STATE:SSM_WIDTH + (SSM_GROUPS + g + 1) * SSM_STATE]
        hn = db * s_ref[gs, :] + xb * bm
        so_ref[gs, :] = hn
        ycol = jnp.sum(hn * cm, axis=-1, keepdims=True)
        yt_ref[gs, :] = jnp.where(lane == b, ycol, yt_ref[gs, :])


def _ssd_sample_post_kernel(yt_ref, xc_ref, z_ref, dexp_ref, ng_ref, o_ref):
    yraw = yt_ref[...].T
    for g in range(SSM_GROUPS):
        gs = slice(g * GROUP_WIDTH, (g + 1) * GROUP_WIDTH)
        y = yraw[:, gs] + dexp_ref[:, gs] * xc_ref[:, gs]
        gated = y * _silu(z_ref[:, gs])
        ms = jnp.mean(gated * gated, axis=-1, keepdims=True)
        o_ref[:, gs] = gated * lax.rsqrt(ms + NORM_EPS) * ng_ref[:, gs]


def _ssd_sample(xbc, z, dt_raw, state_conv, state_ssm, conv_w, conv_b, dt_bias, a_log, d_skip, norm_g):
    nb = xbc.shape[0]
    rows = SSM_HEADS * SSM_HEAD_DIM
    sc = jnp.swapaxes(state_conv, 0, 1)
    st = state_ssm.reshape(nb * rows, SSM_STATE)
    xc, cn, xdt_t, dec_t = pl.pallas_call(
        _ssd_sample_pre_kernel,
        out_shape=[jax.ShapeDtypeStruct((nb, SSM_CONV_DIM), F32),
                   jax.ShapeDtypeStruct((SSM_CONV - 1, nb, SSM_CONV_DIM), F32),
                   jax.ShapeDtypeStruct((SSM_WIDTH, nb), F32),
                   jax.ShapeDtypeStruct((SSM_WIDTH, nb), F32)],
        compiler_params=pltpu.CompilerParams(vmem_limit_bytes=VMEM_LIMIT),
        name="ssd_sample_pre",
    )(xbc, dt_raw, sc, conv_w, conv_b.reshape(1, -1), _pad_lanes(dt_bias),
      jnp.repeat(a_log, SSM_HEAD_DIM).reshape(1, -1), _head_expand_matrix())

    def full(shape):
        return pl.BlockSpec(shape, lambda i: tuple(0 for _ in shape))

    so, yt = pl.pallas_call(
        _ssd_sample_state_kernel,
        grid=(nb,),
        in_specs=[full((SSM_WIDTH, nb)), full((SSM_WIDTH, nb)), full((nb, SSM_CONV_DIM)),
                  pl.BlockSpec((rows, SSM_STATE), lambda i: (i, 0))],
        out_specs=[pl.BlockSpec((rows, SSM_STATE), lambda i: (i, 0)), full((SSM_WIDTH, nb))],
        out_shape=[jax.ShapeDtypeStruct(st.shape, F32),
                   jax.ShapeDtypeStruct((SSM_WIDTH, nb), F32)],
        compiler_params=_params(("arbitrary",)),
        name="ssd_sample_state",
    )(xdt_t, dec_t, xc, st)
    y = pl.pallas_call(
        _ssd_sample_post_kernel,
        out_shape=jax.ShapeDtypeStruct((nb, SSM_WIDTH), F32),
        compiler_params=pltpu.CompilerParams(vmem_limit_bytes=VMEM_LIMIT),
        name="ssd_sample_post",
    )(yt, xc, z, jnp.repeat(d_skip, SSM_HEAD_DIM).reshape(1, -1), norm_g.reshape(1, -1))
    return y, jnp.swapaxes(cn, 0, 1), so.reshape(state_ssm.shape)


def _peer_query_kernel(h_ref, g_ref, wqt_ref, xnt_ref, qt_ref, xnt_scr):
    j = pl.program_id(1)

    @pl.when(j == 0)
    def _():
        x = h_ref[...]
        ms = jnp.mean(x * x, axis=-1, keepdims=True)
        xn = x * lax.rsqrt(ms + NORM_EPS) * g_ref[...]
        xnt = xn.T.astype(BF16)
        xnt_scr[...] = xnt
        xnt_ref[...] = xnt

    qt_ref[...] = _dot(wqt_ref[...], xnt_scr[...])


def _peer_query(h, g, wqt16, *, tm, tr):
    m, d = h.shape
    nq = wqt16.shape[0]
    return pl.pallas_call(
        _peer_query_kernel,
        grid=(m // tm, nq // tr),
        in_specs=[pl.BlockSpec((tm, d), lambda i, j: (i, 0)),
                  pl.BlockSpec((1, d), lambda i, j: (0, 0)),
                  pl.BlockSpec((tr, d), lambda i, j: (j, 0))],
        out_specs=[pl.BlockSpec((d, tm), lambda i, j: (0, i)),
                   pl.BlockSpec((tr, tm), lambda i, j: (j, i))],
        out_shape=[jax.ShapeDtypeStruct((d, m), BF16),
                   jax.ShapeDtypeStruct((nq, m), F32)],
        scratch_shapes=[pltpu.VMEM((d, tm), BF16)],
        compiler_params=_params(("parallel", "arbitrary")),
        name="peer_query",
    )(h, g, wqt16)


def _peer_retrieve_kernel(qt_ref, keys_ref, s1_ref, c1_ref, thr1_ref, eq1_ref, s2_ref, e2_ref, thr2_ref):
    tb = qt_ref.shape[1]
    nkeys = PEER_KEYS
    iota_n = lax.broadcasted_iota(jnp.int32, (nkeys, tb), 0)
    npair = PEER_TOPK * PEER_TOPK
    iota_p = lax.broadcasted_iota(jnp.int32, (npair, tb), 0)
    for h in range(PEER_HEADS):
        tops, scores, ranks = [], [], []
        for j in range(2):
            hj = 2 * h + j
            s = _dot(keys_ref[hj], qt_ref[hj * PEER_HALF:(hj + 1) * PEER_HALF, :].astype(BF16))
            cur = s
            rank = jnp.full((nkeys, tb), float(PEER_TOPK), F32)
            vals = []
            for k in range(PEER_TOPK):
                m = jnp.max(cur, axis=0, keepdims=True)
                first = jnp.min(jnp.where(cur == m, iota_n, nkeys), axis=0, keepdims=True)
                hit = iota_n == first
                rank = jnp.where(hit, float(k), rank)
                cur = jnp.where(hit, -jnp.inf, cur)
                vals.append(m)
            tops.append(vals)
            scores.append(s)
            ranks.append(rank)
        v2 = jnp.concatenate(tops[1], axis=0)
        cand = jnp.concatenate([tops[0][ka] + v2 for ka in range(PEER_TOPK)], axis=0)
        picked = []
        pos = None
        for k in range(PEER_TOPK):
            m = jnp.max(cand, axis=0, keepdims=True)
            pos = jnp.min(jnp.where(cand == m, iota_p, npair), axis=0, keepdims=True)
            cand = jnp.where(iota_p == pos, -jnp.inf, cand)
            picked.append(m)
        tau = picked[-1]
        tau_up = jnp.full_like(tau, jnp.inf)
        z = jnp.zeros_like(tau)
        for m in picked:
            tau_up = jnp.minimum(tau_up, jnp.where(m > tau, m, jnp.inf))
            z = z + jnp.exp(m - picked[0])
        posf = pos.astype(F32)
        ra = jnp.floor(posf / PEER_TOPK)
        rb = posf - ra * PEER_TOPK
        m1 = tops[0][0]
        m2 = tops[1][0]
        zsum = z * jnp.exp(picked[0] - (m1 + m2))
        s1_ref[h, 0] = scores[0]
        c1_ref[h, 0] = jnp.exp(scores[0] - m1) / zsum
        thr1_ref[h, 0] = jnp.where(ranks[0] < ra, tau, tau_up)
        eq1_ref[h, 0] = jnp.where(ranks[0] == ra, 1.0, 0.0)
        s2_ref[h, 0] = scores[1]
        e2_ref[h, 0] = jnp.exp(scores[1] - m2)
        thr2_ref[h, 0] = jnp.where(ranks[1] <= rb, tau, tau_up)


def _peer_retrieve(qt, keys16):
    m = qt.shape[1]
    tb = LANES
    nhj = 2 * PEER_HEADS
    spec = pl.BlockSpec((PEER_HEADS, 1, PEER_KEYS, tb), lambda i: (0, i, 0, 0))
    shape = jax.ShapeDtypeStruct((PEER_HEADS, m // tb, PEER_KEYS, tb), F32)
    return pl.pallas_call(
        _peer_retrieve_kernel,
        grid=(m // tb,),
        in_specs=[pl.BlockSpec((qt.shape[0], tb), lambda i: (0, i)),
                  pl.BlockSpec((nhj, PEER_KEYS, PEER_HALF), lambda i: (0, 0, 0))],
        out_specs=[spec] * 7,
        out_shape=[shape] * 7,
        compiler_params=_params(("parallel",)),
        name="peer_retrieve",
    )(qt, keys16)


def _gelu(x):
    return 0.5 * x * (1.0 + lax.erf(x * (2.0 ** -0.5)))


def _peer_expert_kernel(xnt_ref, s1_ref, c1_ref, thr1_ref, eq1_ref, s2_ref, e2_ref, thr2_ref,
                        u_ref, vt_ref, o_ref, act0, act1, hd, *, n_eblk, a_per_blk):
    e = pl.program_id(1)
    nk = PEER_KEYS
    tt = xnt_ref.shape[1]

    def u_dot(dst):
        dst[...] = _dot(u_ref[...], xnt_ref[...])

    def mask_gelu(src, ai):
        a = (e - 1) * a_per_blk + ai
        row = pl.ds(a, 1)
        for tl in range(tt // LANES):
            ls = slice(tl * LANES, (tl + 1) * LANES)
            w = jnp.zeros((nk, LANES), F32)
            for h in range(PEER_HEADS):
                thr = jnp.where(eq1_ref[h, tl, row, :] > 0.5, thr2_ref[h, tl], thr1_ref[h, tl, row, :])
                pair = s2_ref[h, tl] + s1_ref[h, tl, row, :]
                w = w + jnp.where(pair >= thr, e2_ref[h, tl] * c1_ref[h, tl, row, :], 0.0)
            hd[ai * nk:(ai + 1) * nk, ls] = (_gelu(src[ai * nk:(ai + 1) * nk, ls]) * w).astype(BF16)

    def skewed(dst, src):
        u_dot(dst)
        for ai in range(a_per_blk):
            mask_gelu(src, ai)
        o_ref[...] += _dot(vt_ref[...], hd[...])

    @pl.when(e == 0)
    def _():
        o_ref[...] = jnp.zeros(o_ref.shape, F32)
        u_dot(act0)

    @pl.when((e > 0) & (e < n_eblk) & (e % 2 == 1))
    def _():
        skewed(act1, act0)

    @pl.when((e > 0) & (e < n_eblk) & (e % 2 == 0))
    def _():
        skewed(act0, act1)

    @pl.when(e == n_eblk)
    def _():
        src = act1 if (n_eblk - 1) % 2 == 1 else act0
        for ai in range(a_per_blk):
            mask_gelu(src, ai)
        o_ref[...] += _dot(vt_ref[...], hd[...])


def _peer_experts(xnt, tiles, u16, vt16, *, tt, eb):
    d, m = xnt.shape
    n_exp = u16.shape[0]
    n_eblk = n_exp // eb
    once = pl.Buffered(1)
    tile_spec = pl.BlockSpec((PEER_HEADS, tt // LANES, PEER_KEYS, LANES), lambda i, e: (0, i, 0, 0),
                             pipeline_mode=once)
    return pl.pallas_call(
        functools.partial(_peer_expert_kernel, n_eblk=n_eblk, a_per_blk=eb // PEER_KEYS),
        grid=(m // tt, n_eblk + 1),
        in_specs=[pl.BlockSpec((d, tt), lambda i, e: (0, i), pipeline_mode=once)]
        + [tile_spec] * 7
        + [pl.BlockSpec((eb, d), lambda i, e: (jnp.minimum(e, n_eblk - 1), 0)),
           pl.BlockSpec((d, eb), lambda i, e: (0, jnp.maximum(e - 1, 0)))],
        out_specs=pl.BlockSpec((d, tt), lambda i, e: (0, i), pipeline_mode=once),
        out_shape=jax.ShapeDtypeStruct((d, m), F32),
        scratch_shapes=[pltpu.VMEM((eb, tt), F32), pltpu.VMEM((eb, tt), F32), pltpu.VMEM((eb, tt), BF16)],
        compiler_params=_params(("parallel", "arbitrary")),
        name="peer_experts",
    )(xnt, *tiles, u16, vt16)


def _final_kernel(h_ref, ft_ref, g_ref, o_ref):
    x = h_ref[...] + ft_ref[...].T
    ms = jnp.mean(x * x, axis=-1, keepdims=True)
    o_ref[...] = x * lax.rsqrt(ms + NORM_EPS) * g_ref[...]


def _final_norm(h, ffn_t, g, *, tm):
    m, d = h.shape
    row = pl.BlockSpec((tm, d), lambda i: (i, 0))
    return pl.pallas_call(
        _final_kernel,
        grid=(m // tm,),
        in_specs=[row, pl.BlockSpec((d, tm), lambda i: (0, i)), pl.BlockSpec((1, d), lambda i: (0, 0))],
        out_specs=row,
        out_shape=jax.ShapeDtypeStruct((m, d), F32),
        compiler_params=_params(("parallel",)),
        name="final_norm",
    )(h, ffn_t, g)


def _tile(m, pref):
    return pref if m % pref == 0 else m


def _in_proj(x, g_mix, w16, wdt16, *, transposed):
    m = x.shape[0]
    tm = _tile(m, 512)
    qw = 2 * ATTN_HEADS * ATTN_HEAD_DIM
    t16 = "bf16T" if transposed else "bf16"
    segs = [(0, qw, ATTN_SCALE, (t16,)), (qw, qw, 1.0, ("f32", "bf16")), (2 * qw, ATTN_WIDTH, 1.0, ("f32", t16)),
            (2 * qw + ATTN_WIDTH, SSM_WIDTH, 1.0, ("f32",)),
            (2 * qw + ATTN_WIDTH + SSM_WIDTH, SSM_CONV_DIM, 1.0, ("f32",))]
    q16, k32, k16, v32, v16, z32, xbc32, xn16 = _norm_mm(x, g_mix, w16, segs, tm=tm, tn=512, emit_xn=True)
    dt_raw = _mm([xn16], [wdt16], None, tm=tm, tn=LANES)
    return q16, k32, k16, v32, v16, z32, xbc32, dt_raw


def _peer_and_final(h, g_ffn, wqt16, keys16, u16, vt16, g_final):
    m = h.shape[0]
    xnt, qt = _peer_query(h, g_ffn, wqt16, tm=_tile(m, 256), tr=512)
    tiles = _peer_retrieve(qt, keys16)
    ffn_t = _peer_experts(xnt, tiles, u16, vt16, tt=_tile(m, 512), eb=512)
    return _final_norm(h, ffn_t, g_final, tm=_tile(m, 256))


def kernel(x_prompt, x_sample, cache_k, cache_v, state_conv, state_ssm, page_table, g_mix, w_in, lambda_q, lambda_k, subln_g, conv_w, conv_b, dt_bias, a_log, d_skip, ssm_norm_g, w_out, g_ffn, peer_wq, peer_keys, peer_u, peer_v, g_final):
    assert w_in.shape[0] == 1, "single trunk layer"
    bsz, seq, dm = x_prompt.shape
    nb = x_sample.shape[0]
    assert bsz == 1 and x_sample.shape[1] == 1
    main_cols = w_in.shape[2] - SSM_HEADS
    w16 = w_in[0].astype(BF16)
    wdt16 = jnp.pad(w_in[0][:, main_cols:], ((0, 0), (0, LANES - SSM_HEADS))).astype(BF16)
    wo16 = w_out[0].astype(BF16)
    wqt16 = peer_wq[0].T.astype(BF16)
    keys16 = peer_keys[0].reshape(2 * PEER_HEADS, PEER_KEYS, PEER_HALF).astype(BF16)
    u16 = peer_u[0].astype(BF16)
    vt16 = peer_v[0].T.astype(BF16)
    gm = g_mix[0].reshape(1, -1)
    gf = g_ffn[0].reshape(1, -1)
    gfin = g_final.reshape(1, -1)
    sg = subln_g[0].reshape(1, -1)
    lq, lk = lambda_q[0], lambda_k[0]

    xp = x_prompt.reshape(seq, dm)
    qt16, k32, k16, v32, vtok16, z32, xbc32, dtr = _in_proj(xp, gm, w16, wdt16, transposed=True)
    attn16 = _prompt_attention(qt16, k16, vtok16, lq, lk, sg, t=_tile(seq, 512)).T
    y16, hfin = _ssd_prompt(xbc32, z32, dtr, conv_w[0], conv_b[0], dt_bias[0], a_log[0], d_skip[0],
                            ssm_norm_g[0])
    hp = _mm([attn16, y16], [wo16[:ATTN_WIDTH], wo16[ATTN_WIDTH:]], xp, tm=_tile(seq, 512), tn=512)
    y_prompt = _peer_and_final(hp, gf, wqt16, keys16, u16, vt16, gfin)

    xs = x_sample.reshape(nb, dm)
    sq16, sk32, _, sv32, _, sz32, sxbc32, sdtr = _in_proj(xs, gm, w16, wdt16, transposed=False)
    sattn = _sample_attention(sq16.astype(F32), sk32, sv32, cache_k[0], cache_v[0], page_table, lq, lk, sg)
    sy, conv_s, ssm_s = _ssd_sample(sxbc32, sz32, sdtr, state_conv[0], state_ssm[0], conv_w[0], conv_b[0],
                                    dt_bias[0], a_log[0], d_skip[0], ssm_norm_g[0])
    hs = _mm([sattn.astype(BF16), sy.astype(BF16)], [wo16[:ATTN_WIDTH], wo16[ATTN_WIDTH:]], xs,
             tm=nb, tn=512)
    y_sample = _peer_and_final(hs, gf, wqt16, keys16, u16, vt16, gfin)

    return (y_prompt.reshape(bsz, seq, dm),
            y_sample.reshape(nb, 1, dm),
            k32.reshape(1, bsz, seq, ATTN_HEADS, 2, ATTN_HEAD_DIM),
            v32.reshape(1, bsz, seq, ATTN_HEADS, ATTN_V_DIM),
            xbc32[seq - (SSM_CONV - 1):].reshape(1, bsz, SSM_CONV - 1, SSM_CONV_DIM),
            hfin.reshape(1, bsz, SSM_HEADS, SSM_HEAD_DIM, SSM_STATE),
            sk32.reshape(1, nb, 1, ATTN_HEADS, 2, ATTN_HEAD_DIM),
            sv32.reshape(1, nb, 1, ATTN_HEADS, ATTN_V_DIM),
            conv_s.reshape(1, nb, SSM_CONV - 1, SSM_CONV_DIM),
            ssm_s.reshape(1, nb, SSM_HEADS, SSM_HEAD_DIM, SSM_STATE))
```

```python
import functools
import math

import jax
import jax.numpy as jnp
import numpy as np
from jax import lax
from jax.experimental import pallas as pl
from jax.experimental.pallas import tpu as pltpu

F32 = jnp.float32
BF16 = jnp.bfloat16

ATTN_HEADS = 8
ATTN_HEAD_DIM = 128
ATTN_V_DIM = 256
ATTN_WIDTH = ATTN_HEADS * ATTN_V_DIM
SSM_WIDTH = 2048
SSM_HEAD_DIM = 64
SSM_HEADS = 32
SSM_GROUPS = 8
SSM_STATE = 128
SSM_CONV = 4
SSM_CHUNK = 128
SSM_CONV_DIM = SSM_WIDTH + 2 * SSM_GROUPS * SSM_STATE
GROUP_WIDTH = SSM_WIDTH // SSM_GROUPS
PEER_HEADS = 8
PEER_KEYS = 128
PEER_HALF = 128
PEER_TOPK = 16
NORM_EPS = 1e-6
LAM_INIT = 0.8 - 0.6 * math.exp(-0.3 * 0)
ATTN_SCALE = ATTN_HEAD_DIM ** -0.5
LANES = 128
VMEM_LIMIT = 56 * 1024 * 1024


def _params(sem):
    return pltpu.CompilerParams(dimension_semantics=sem, vmem_limit_bytes=VMEM_LIMIT)


def _split3(x):
    x1 = x.astype(BF16)
    r1 = x - x1.astype(F32)
    x2 = r1.astype(BF16)
    x3 = (r1 - x2.astype(F32)).astype(BF16)
    return x1, x2, x3


def _dot(a, b):
    return jnp.dot(a, b, preferred_element_type=F32)


def _dot_nt(a, b):
    return lax.dot_general(a, b, (((1,), (1,)), ((), ())), preferred_element_type=F32)


def _dot_exact_rhs(x, w16):
    x1, x2, x3 = _split3(x)
    return _dot(x1, w16) + _dot(x2, w16) + _dot(x3, w16)


def _dot_exact_lhs(w16, x):
    x1, x2, x3 = _split3(x)
    return _dot(w16, x1) + _dot(w16, x2) + _dot(w16, x3)


def _silu(x):
    return x / (1.0 + jnp.exp(-x))


def _softplus(x):
    return jnp.maximum(x, 0.0) + jnp.log1p(jnp.exp(-jnp.abs(x)))


def _lambda(lq_ref, lk_ref):
    e = jnp.exp(jnp.sum(lq_ref[...] * lk_ref[...], axis=-1, keepdims=True))
    return e[0:1, :] - e[1:2, :] + LAM_INIT


def _alibi_slopes():
    return 2.0 ** (-8.0 * np.arange(1, ATTN_HEADS + 1) / ATTN_HEADS)


def _norm_mm_kernel(x_ref, g_ref, w_ref, *rest, segs, emit_xn):
    outs = list(rest[:-1])
    xn_scr = rest[-1]
    j = pl.program_id(1)

    @pl.when(j == 0)
    def _():
        x = x_ref[...]
        ms = jnp.mean(x * x, axis=-1, keepdims=True)
        xn = (x * lax.rsqrt(ms + NORM_EPS) * g_ref[...]).astype(BF16)
        xn_scr[...] = xn
        if emit_xn:
            outs[-1][...] = xn

    acc = _dot(xn_scr[...], w_ref[...])
    k = 0
    for (start, nblk, scale, kinds) in segs:
        refs = [(outs[k + i], kind) for i, kind in enumerate(kinds)]
        k += len(kinds)

        @pl.when((j >= start) & (j < start + nblk))
        def _(refs=refs, scale=scale):
            val = acc if scale == 1.0 else acc * scale
            for r, kind in refs:
                r[...] = (val.T if kind.endswith("T") else val).astype(r.dtype)


def _norm_mm(x, g, w16, segs, *, tm, tn, emit_xn):
    m, kdim = x.shape
    nblocks = sum(wd for _, wd, _, _ in segs) // tn
    bsegs = [(s // tn, wd // tn, scale, kinds) for s, wd, scale, kinds in segs]
    out_shapes, out_specs = [], []
    for (sb, nb, _, kinds), (_, wd, _, _) in zip(bsegs, segs):
        def imap(i, j, sb=sb, nb=nb):
            return (i, jnp.clip(j - sb, 0, nb - 1))

        def imap_t(i, j, sb=sb, nb=nb):
            return (jnp.clip(j - sb, 0, nb - 1), i)
        for kind in kinds:
            dt = F32 if kind == "f32" else BF16
            if kind.endswith("T"):
                out_shapes.append(jax.ShapeDtypeStruct((wd, m), dt))
                out_specs.append(pl.BlockSpec((tn, tm), imap_t))
            else:
                out_shapes.append(jax.ShapeDtypeStruct((m, wd), dt))
                out_specs.append(pl.BlockSpec((tm, tn), imap))
    if emit_xn:
        out_shapes.append(jax.ShapeDtypeStruct((m, kdim), BF16))
        out_specs.append(pl.BlockSpec((tm, kdim), lambda i, j: (i, 0)))
    return pl.pallas_call(
        functools.partial(_norm_mm_kernel, segs=bsegs, emit_xn=emit_xn),
        grid=(m // tm, nblocks),
        in_specs=[pl.BlockSpec((tm, kdim), lambda i, j: (i, 0), pipeline_mode=pl.Buffered(1)),
                  pl.BlockSpec((1, kdim), lambda i, j: (0, 0)),
                  pl.BlockSpec((kdim, tn), lambda i, j: (0, j))],
        out_specs=out_specs,
        out_shape=out_shapes,
        scratch_shapes=[pltpu.VMEM((tm, kdim), BF16)],
        compiler_params=_params(("parallel", "arbitrary")),
        name="norm_mm",
    )(x, g, w16)


def _mm_kernel(*refs, n_pairs, has_res):
    o_ref = refs[-1]
    acc = _dot(refs[0][...], refs[n_pairs][...])
    for i in range(1, n_pairs):
        acc = acc + _dot(refs[i][...], refs[n_pairs + i][...])
    if has_res:
        acc = acc + refs[2 * n_pairs][...]
    o_ref[...] = acc


def _mm(a_list, w_list, res, *, tm, tn):
    m = a_list[0].shape[0]
    n = w_list[0].shape[1]
    in_specs = [pl.BlockSpec((tm, a.shape[1]), lambda i, j: (i, 0)) for a in a_list]
    in_specs += [pl.BlockSpec((w.shape[0], tn), lambda i, j: (0, j)) for w in w_list]
    args = list(a_list) + list(w_list)
    if res is not None:
        in_specs.append(pl.BlockSpec((tm, tn), lambda i, j: (i, j)))
        args.append(res)
    return pl.pallas_call(
        functools.partial(_mm_kernel, n_pairs=len(a_list), has_res=res is not None),
        grid=(m // tm, n // tn),
        in_specs=in_specs,
        out_specs=pl.BlockSpec((tm, tn), lambda i, j: (i, j)),
        out_shape=jax.ShapeDtypeStruct((m, n), F32),
        compiler_params=_params(("parallel", "parallel")),
        name="mm",
    )(*args)


def _attn_kernel(qi_tab, kj_tab, slopes_ref, qt_ref, k_ref, vt_ref, lq_ref, lk_ref, g_ref, o_ref,
                 m_scr, l_scr, acc_scr, *, t):
    h = pl.program_id(0)
    step = pl.program_id(1)
    qi = qi_tab[step]
    kj = kj_tab[step]
    nrep = t // LANES

    @pl.when(kj == 0)
    def _():
        m_scr[...] = jnp.full(m_scr.shape, -jnp.inf, F32)
        l_scr[...] = jnp.zeros(l_scr.shape, F32)
        acc_scr[...] = jnp.zeros(acc_scr.shape, F32)

    def update(masked):
        krow = lax.broadcasted_iota(jnp.int32, (t, LANES), 0)
        kbias = slopes_ref[h] * (krow + (kj - qi) * t).astype(F32)
        bias = jnp.concatenate([kbias] * nrep, axis=1)
        qt = qt_ref[...]
        k = k_ref[...]
        vt = vt_ref[...]
        if masked:
            causal = (lax.broadcasted_iota(jnp.int32, (t, t), 1)
                      >= lax.broadcasted_iota(jnp.int32, (t, t), 0))
        for j in range(2):
            s = _dot(k[:, j * 128:(j + 1) * 128], qt[j * 128:(j + 1) * 128, :]) + bias
            if masked:
                s = jnp.where(causal, s, -jnp.inf)
            m_prev = m_scr[j]
            m_new = jnp.maximum(m_prev, jnp.max(s, axis=0, keepdims=True))
            alpha = jnp.exp(m_prev - m_new)
            p = jnp.exp(s - m_new)
            l_scr[j] = alpha * l_scr[j] + jnp.sum(p, axis=0, keepdims=True)
            acc_scr[j] = alpha * acc_scr[j] + _dot(vt, p.astype(BF16))
            m_scr[j] = m_new

    @pl.when(kj < qi)
    def _():
        update(False)

    @pl.when(kj == qi)
    def _():
        update(True)
        lam = _lambda(lq_ref, lk_ref)
        d = acc_scr[0] / l_scr[0] - lam * (acc_scr[1] / l_scr[1])
        ms = jnp.mean(d * d, axis=0, keepdims=True)
        g = jnp.concatenate([g_ref[...]] * nrep, axis=1)
        o_ref[...] = (d * lax.rsqrt(ms + NORM_EPS) * g * (1.0 - LAM_INIT)).astype(o_ref.dtype)


def _prompt_attention(qt16, k16, vt16, lam_q, lam_k, subln_g, *, t):
    l = k16.shape[0]
    nq = l // t
    pairs = [(qi, kj) for qi in range(nq) for kj in range(qi + 1)]
    qi_tab = jnp.asarray([p[0] for p in pairs], jnp.int32)
    kj_tab = jnp.asarray([p[1] for p in pairs], jnp.int32)
    slopes = jnp.asarray(_alibi_slopes(), F32)
    g_tile = jnp.broadcast_to(subln_g.reshape(ATTN_V_DIM, 1), (ATTN_V_DIM, LANES))

    def full(shape):
        return pl.BlockSpec(shape, lambda h, s, qt, kt: tuple(0 for _ in shape))

    return pl.pallas_call(
        functools.partial(_attn_kernel, t=t),
        grid_spec=pltpu.PrefetchScalarGridSpec(
            num_scalar_prefetch=2,
            grid=(ATTN_HEADS, len(pairs)),
            in_specs=[pl.BlockSpec(memory_space=pltpu.SMEM),
                      pl.BlockSpec((256, t), lambda h, s, qt, kt: (h, qt[s])),
                      pl.BlockSpec((t, 256), lambda h, s, qt, kt: (kt[s], h)),
                      pl.BlockSpec((256, t), lambda h, s, qt, kt: (h, kt[s])),
                      full((2, 128)), full((2, 128)), full((ATTN_V_DIM, LANES))],
            out_specs=pl.BlockSpec((256, t), lambda h, s, qt, kt: (h, qt[s])),
            scratch_shapes=[pltpu.VMEM((2, 1, t), F32),
                            pltpu.VMEM((2, 1, t), F32),
                            pltpu.VMEM((2, ATTN_V_DIM, t), F32)]),
        out_shape=jax.ShapeDtypeStruct((ATTN_WIDTH, l), BF16),
        compiler_params=_params(("parallel", "arbitrary")),
        name="prompt_attn",
    )(qi_tab, kj_tab, slopes, qt16, k16, vt16, lam_q, lam_k, g_tile)


def _rows_from_lanes(row, n, width):
    return jnp.concatenate([row[:, i * width:(i + 1) * width] for i in range(n)], axis=0)


def _map_major_rows(row):
    w = ATTN_HEAD_DIM
    return jnp.concatenate([row[:, (h * 2 + j) * w:(h * 2 + j + 1) * w]
                            for j in range(2) for h in range(ATTN_HEADS)], axis=0)


def _sample_attn_kernel(pt_ref, q_ref, kn_ref, vn_ref, *rest, n_steps, pages_per_step, page, past):
    del pt_ref
    g_pages = pages_per_step
    k_refs = rest[:g_pages]
    v_refs = rest[g_pages:2 * g_pages]
    slope_ref, lq_ref, lk_ref, g_ref, o_ref, q_scr, m_scr, l_scr, acc_scr = rest[2 * g_pages:]
    p = pl.program_id(1)
    nh = ATTN_HEADS
    rows = page * nh

    @pl.when(p == 0)
    def _():
        q_scr[...] = _map_major_rows(q_ref[0]).astype(BF16)
        m_scr[...] = jnp.full(m_scr.shape, -jnp.inf, F32)
        l_scr[...] = jnp.zeros(l_scr.shape, F32)
        acc_scr[...] = jnp.zeros(acc_scr.shape, F32)

    lane = lax.broadcasted_iota(jnp.int32, (nh, rows), 1)
    own = lane % nh == lax.broadcasted_iota(jnp.int32, (nh, rows), 0)
    slope = slope_ref[...][:, :1]
    scores = []
    for gi in range(g_pages):
        kpos = (p * g_pages + gi) * page + lane // nh
        bias = slope * (past - kpos).astype(F32)
        sj = []
        for j in range(2):
            kj = k_refs[gi][pl.ds(j, rows, stride=2), :].astype(BF16)
            s = _dot_nt(q_scr[j * nh:(j + 1) * nh, :], kj) - bias
            sj.append(jnp.where(own, s, -jnp.inf))
        scores.append(jnp.concatenate(sj, axis=0))
    m_prev = m_scr[...]
    m_new = m_prev
    for s in scores:
        m_new = jnp.maximum(m_new, jnp.max(s, axis=-1, keepdims=True))
    alpha = jnp.exp(m_prev - m_new)
    l_new = alpha * l_scr[...]
    acc = alpha[:, :1] * acc_scr[...]
    for gi, s in enumerate(scores):
        pe = jnp.exp(s - m_new[:, :1])
        l_new = l_new + jnp.sum(pe, axis=-1, keepdims=True)
        acc = acc + _dot(pe.astype(BF16), v_refs[gi][...].astype(BF16))
    m_scr[...] = m_new
    l_scr[...] = l_new
    acc_scr[...] = acc

    @pl.when(p == n_steps - 1)
    def _():
        kn = _map_major_rows(kn_ref[0]).astype(BF16).astype(F32)
        vn = _rows_from_lanes(vn_ref[0], nh, ATTN_V_DIM).astype(BF16).astype(F32)
        vn2 = jnp.concatenate([vn, vn], axis=0)
        s_self = jnp.sum(q_scr[...].astype(F32) * kn, axis=-1, keepdims=True)
        m_prev = m_scr[...]
        m_new = jnp.maximum(m_prev, s_self)
        alpha = jnp.exp(m_prev - m_new)
        p_self = jnp.exp(s_self - m_new)
        l_fin = alpha * l_scr[...] + p_self
        acc = alpha[:, :1] * acc_scr[...] + p_self[:, :1] * vn2
        n = acc / l_fin[:, :1]
        d = n[0:nh, :] - _lambda(lq_ref, lk_ref) * n[nh:2 * nh, :]
        ms = jnp.mean(d * d, axis=-1, keepdims=True)
        o_ref[0] = d * lax.rsqrt(ms + NORM_EPS) * g_ref[...] * (1.0 - LAM_INIT)


def _sample_attention(q, k_new, v_new, cache_k, cache_v, page_table, lam_q, lam_k, subln_g):
    b = q.shape[0]
    page = cache_k.shape[1]
    n_pages = page_table.shape[1]
    width = q.shape[1]
    ck = cache_k.reshape(-1, ATTN_HEAD_DIM)
    cv = cache_v.reshape(-1, ATTN_V_DIM)
    g_pages = math.gcd(n_pages, 8)
    n_steps = n_pages // g_pages
    slope_tile = jnp.asarray(np.repeat(_alibi_slopes()[:, None], LANES, axis=1), F32)
    row_spec = pl.BlockSpec((1, 1, width), lambda i, p, pt: (i, 0, 0))

    def page_spec(rows, lanes, gi):
        return pl.BlockSpec((rows, lanes), lambda i, p, pt: (pt[i * n_pages + p * g_pages + gi], 0))

    def full(shape):
        return pl.BlockSpec(shape, lambda i, p, pt: tuple(0 for _ in shape))

    out = pl.pallas_call(
        functools.partial(_sample_attn_kernel, n_steps=n_steps, pages_per_step=g_pages, page=page,
                          past=n_pages * page),
        grid_spec=pltpu.PrefetchScalarGridSpec(
            num_scalar_prefetch=1,
            grid=(b, n_steps),
            in_specs=[row_spec, row_spec, row_spec]
            + [page_spec(page * ATTN_HEADS * 2, ATTN_HEAD_DIM, gi) for gi in range(g_pages)]
            + [page_spec(page * ATTN_HEADS, ATTN_V_DIM, gi) for gi in range(g_pages)]
            + [full((ATTN_HEADS, LANES)), full((2, 128)), full((2, 128)), full((1, ATTN_V_DIM))],
            out_specs=pl.BlockSpec((1, ATTN_HEADS, ATTN_V_DIM), lambda i, p, pt: (i, 0, 0)),
            scratch_shapes=[pltpu.VMEM((2 * ATTN_HEADS, ATTN_HEAD_DIM), BF16),
                            pltpu.VMEM((2 * ATTN_HEADS, LANES), F32),
                            pltpu.VMEM((2 * ATTN_HEADS, LANES), F32),
                            pltpu.VMEM((2 * ATTN_HEADS, ATTN_V_DIM), F32)]),
        out_shape=jax.ShapeDtypeStruct((b, ATTN_HEADS, ATTN_V_DIM), F32),
        compiler_params=_params(("parallel", "arbitrary")),
        name="sample_attn",
    )(page_table.reshape(-1), q.reshape(b, 1, width), k_new.reshape(b, 1, width),
      v_new.reshape(b, 1, width), *([ck] * g_pages), *([cv] * g_pages), slope_tile, lam_q, lam_k, subln_g)
    return out.reshape(b, ATTN_WIDTH)


def _ssd_prompt_kernel(xbc_ref, z_ref, dtr_ref, cw_ref, cb_ref, dtb_ref, alog_ref, dexp_ref, ng_ref,
                       e_ref, y_ref, hfin_ref, buf, xc, ht, *, n_chunks):
    c = pl.program_id(0)
    q = SSM_CHUNK

    @pl.when(c == 0)
    def _():
        buf[0:8, :] = jnp.zeros((8, SSM_CONV_DIM), F32)
        ht[...] = jnp.zeros(ht.shape, F32)

    buf[8:8 + q, :] = xbc_ref[...]
    blk = 512
    for cbk in range(SSM_CONV_DIM // blk):
        sl = slice(cbk * blk, (cbk + 1) * blk)
        acc = cb_ref[:, sl]
        for j in range(SSM_CONV):
            acc = acc + cw_ref[j:j + 1, sl] * buf[5 + j:5 + j + q, sl]
        xc[:, sl] = _silu(acc)
    buf[0:8, :] = buf[q:q + 8, :]

    dt = _softplus(dtr_ref[...] + dtb_ref[...])
    a = dt * (-jnp.exp(alog_ref[...]))
    ri = lax.broadcasted_iota(jnp.int32, (q, q), 0)
    ci = lax.broadcasted_iota(jnp.int32, (q, q), 1)
    causal = ri >= ci
    tril = jnp.where(causal, 1.0, 0.0).astype(BF16)
    a_cum = _dot_exact_lhs(tril, a)
    a_cum_t = a_cum.T
    e16 = e_ref[...]
    ac_exp = _dot_exact_rhs(a_cum, e16)
    dt_exp = _dot_exact_rhs(dt, e16)
    ac_last = ac_exp[q - 1:q, :]
    xs = xc[:, 0:SSM_WIDTH]
    xdt = xs * dt_exp
    xw = xdt * jnp.exp(ac_last - ac_exp)
    ea = jnp.exp(ac_exp)
    cdec = jnp.exp(ac_last)
    lane_g = lax.broadcasted_iota(jnp.int32, (q, GROUP_WIDTH), 1)

    for g in range(SSM_GROUPS):
        gs = slice(g * GROUP_WIDTH, (g + 1) * GROUP_WIDTH)
        bm = xc[:, SSM_WIDTH + g * SSM_STATE:SSM_WIDTH + (g + 1) * SSM_STATE]
        cm = xc[:, SSM_WIDTH + (SSM_GROUPS + g) * SSM_STATE:SSM_WIDTH + (SSM_GROUPS + g + 1) * SSM_STATE]
        bm16 = bm.astype(BF16)
        cm16 = cm.astype(BF16)
        cbm = _dot_nt(cm16, bm16)
        xdt16 = xdt[:, gs].astype(BF16)
        yg = jnp.zeros((q, GROUP_WIDTH), F32)
        for r in range(SSM_HEADS // SSM_GROUPS):
            hd = g * (SSM_HEADS // SSM_GROUPS) + r
            seg = a_cum[:, hd:hd + 1] - a_cum_t[hd:hd + 1, :]
            dec = jnp.exp(jnp.where(causal, seg, -jnp.inf))
            yr = _dot((cbm * dec).astype(BF16), xdt16)
            yg = jnp.where(lane_g // SSM_HEAD_DIM == r, yr, yg)
        h_in = ht[:, gs]
        y_off = _dot(cm16, h_in.astype(BF16)) * ea[:, gs]
        st = _dot(bm.T.astype(BF16), xw[:, gs].astype(BF16))
        ht[:, gs] = cdec[:, gs] * h_in + st
        y = yg + y_off + dexp_ref[:, gs] * xs[:, gs]
        gated = y * _silu(z_ref[:, gs])
        ms = jnp.mean(gated * gated, axis=-1, keepdims=True)
        y_ref[:, gs] = (gated * lax.rsqrt(ms + NORM_EPS) * ng_ref[:, gs]).astype(y_ref.dtype)

    @pl.when(c == n_chunks - 1)
    def _():
        hfin_ref[...] = ht[...].T


def _head_expand_matrix():
    e = np.zeros((LANES, SSM_WIDTH), np.float32)
    for hd in range(SSM_HEADS):
        e[hd, hd * SSM_HEAD_DIM:(hd + 1) * SSM_HEAD_DIM] = 1.0
    return jnp.asarray(e, BF16)


def _pad_lanes(v):
    return jnp.pad(v.reshape(1, -1), ((0, 0), (0, LANES - v.size)))


def _ssd_prompt(xbc, z, dt_raw, conv_w, conv_b, dt_bias, a_log, d_skip, norm_g):
    l = xbc.shape[0]
    n_chunks = l // SSM_CHUNK
    q = SSM_CHUNK

    def full(shape):
        return pl.BlockSpec(shape, lambda c: tuple(0 for _ in shape))

    y16, hfin = pl.pallas_call(
        functools.partial(_ssd_prompt_kernel, n_chunks=n_chunks),
        grid=(n_chunks,),
        in_specs=[pl.BlockSpec((q, SSM_CONV_DIM), lambda c: (c, 0)),
                  pl.BlockSpec((q, SSM_WIDTH), lambda c: (c, 0)),
                  pl.BlockSpec((q, LANES), lambda c: (c, 0)),
                  full((SSM_CONV, SSM_CONV_DIM)), full((1, SSM_CONV_DIM)),
                  full((1, LANES)), full((1, LANES)), full((1, SSM_WIDTH)), full((1, SSM_WIDTH)),
                  full((LANES, SSM_WIDTH))],
        out_specs=[pl.BlockSpec((q, SSM_WIDTH), lambda c: (c, 0)),
                   full((SSM_WIDTH, SSM_STATE))],
        out_shape=[jax.ShapeDtypeStruct((l, SSM_WIDTH), BF16),
                   jax.ShapeDtypeStruct((SSM_WIDTH, SSM_STATE), F32)],
        scratch_shapes=[pltpu.VMEM((q + 8, SSM_CONV_DIM), F32),
                        pltpu.VMEM((q, SSM_CONV_DIM), F32),
                        pltpu.VMEM((SSM_STATE, SSM_WIDTH), F32)],
        compiler_params=_params(("arbitrary",)),
        name="ssd_prompt",
    )(xbc, z, dt_raw, conv_w, conv_b.reshape(1, -1), _pad_lanes(dt_bias), _pad_lanes(a_log),
      jnp.repeat(d_skip, SSM_HEAD_DIM).reshape(1, -1), norm_g.reshape(1, -1), _head_expand_matrix())
    return y16, hfin


def _ssd_sample_pre_kernel(xbc_ref, dtr_ref, sc_ref, cw_ref, cb_ref, dtb_ref, aexp_ref, e_ref,
                           xc_ref, cn_ref, xdt_t_ref, dec_t_ref):
    xbc = xbc_ref[...]
    acc = cb_ref[...] + cw_ref[SSM_CONV - 1:SSM_CONV, :] * xbc
    for j in range(SSM_CONV - 1):
        acc = acc + cw_ref[j:j + 1, :] * sc_ref[j]
    xcv = _silu(acc)
    xc_ref[...] = xcv
    cn_ref[0] = sc_ref[1]
    cn_ref[1] = sc_ref[2]
    cn_ref[2] = xbc
    dt = _softplus(dtr_ref[...] + dtb_ref[...])
    dt_exp = _dot_exact_rhs(dt, e_ref[...])
    dec_t_ref[...] = jnp.exp(dt_exp * (-jnp.exp(aexp_ref[...]))).T
    xdt_t_ref[...] = (xcv[:, 0:SSM_WIDTH] * dt_exp).T


def _ssd_sample_state_kernel(xdt_t_ref, dec_t_ref, xc_ref, s_ref, so_ref, yt_ref):
    b = pl.program_id(0)
    nb = xdt_t_ref.shape[1]
    onehot = jnp.where(lax.broadcasted_iota(jnp.int32, (nb, SSM_STATE), 0) == b, 1.0, 0.0).astype(BF16)
    lane = lax.broadcasted_iota(jnp.int32, (GROUP_WIDTH, nb), 1)

    @pl.when(b == 0)
    def _():
        yt_ref[...] = jnp.zeros(yt_ref.shape, F32)

    xc_row = xc_ref[pl.ds(b, 1), :]
    for g in range(SSM_GROUPS):
        gs = slice(g * GROUP_WIDTH, (g + 1) * GROUP_WIDTH)
        xb = _dot_exact_rhs(xdt_t_ref[gs, :], onehot)
        db = _dot_exact_rhs(dec_t_ref[gs, :], onehot)
        bm = xc_row[:, SSM_WIDTH + g * SSM_STATE:SSM_WIDTH + (g + 1) * SSM_STATE]
        cm = xc_row[:, SSM_WIDTH + (SSM_GROUPS + g) * SSM_STATE:SSM_WIDTH + (SSM_GROUPS + g + 1) * SSM_STATE]
        hn = db * s_ref[gs, :] + xb * bm
        so_ref[gs, :] = hn
        ycol = jnp.sum(hn * cm, axis=-1, keepdims=True)
        yt_ref[gs, :] = jnp.where(lane == b, ycol, yt_ref[gs, :])


def _ssd_sample_post_kernel(yt_ref, xc_ref, z_ref, dexp_ref, ng_ref, o_ref):
    yraw = yt_ref[...].T
    for g in range(SSM_GROUPS):
        gs = slice(g * GROUP_WIDTH, (g + 1) * GROUP_WIDTH)
        y = yraw[:, gs] + dexp_ref[:, gs] * xc_ref[:, gs]
        gated = y * _silu(z_ref[:, gs])
        ms = jnp.mean(gated * gated, axis=-1, keepdims=True)
        o_ref[:, gs] = gated * lax.rsqrt(ms + NORM_EPS) * ng_ref[:, gs]


def _ssd_sample(xbc, z, dt_raw, state_conv, state_ssm, conv_w, conv_b, dt_bias, a_log, d_skip, norm_g):
    nb = xbc.shape[0]
    rows = SSM_HEADS * SSM_HEAD_DIM
    sc = jnp.swapaxes(state_conv, 0, 1)
    st = state_ssm.reshape(nb * rows, SSM_STATE)
    xc, cn, xdt_t, dec_t = pl.pallas_call(
        _ssd_sample_pre_kernel,
        out_shape=[jax.ShapeDtypeStruct((nb, SSM_CONV_DIM), F32),
                   jax.ShapeDtypeStruct((SSM_CONV - 1, nb, SSM_CONV_DIM), F32),
                   jax.ShapeDtypeStruct((SSM_WIDTH, nb), F32),
                   jax.ShapeDtypeStruct((SSM_WIDTH, nb), F32)],
        compiler_params=pltpu.CompilerParams(vmem_limit_bytes=VMEM_LIMIT),
        name="ssd_sample_pre",
    )(xbc, dt_raw, sc, conv_w, conv_b.reshape(1, -1), _pad_lanes(dt_bias),
      jnp.repeat(a_log, SSM_HEAD_DIM).reshape(1, -1), _head_expand_matrix())

    def full(shape):
        return pl.BlockSpec(shape, lambda i: tuple(0 for _ in shape))

    so, yt = pl.pallas_call(
        _ssd_sample_state_kernel,
        grid=(nb,),
        in_specs=[full((SSM_WIDTH, nb)), full((SSM_WIDTH, nb)), full((nb, SSM_CONV_DIM)),
                  pl.BlockSpec((rows, SSM_STATE), lambda i: (i, 0))],
        out_specs=[pl.BlockSpec((rows, SSM_STATE), lambda i: (i, 0)), full((SSM_WIDTH, nb))],
        out_shape=[jax.ShapeDtypeStruct(st.shape, F32),
                   jax.ShapeDtypeStruct((SSM_WIDTH, nb), F32)],
        compiler_params=_params(("arbitrary",)),
        name="ssd_sample_state",
    )(xdt_t, dec_t, xc, st)
    y = pl.pallas_call(
        _ssd_sample_post_kernel,
        out_shape=jax.ShapeDtypeStruct((nb, SSM_WIDTH), F32),
        compiler_params=pltpu.CompilerParams(vmem_limit_bytes=VMEM_LIMIT),
        name="ssd_sample_post",
    )(yt, xc, z, jnp.repeat(d_skip, SSM_HEAD_DIM).reshape(1, -1), norm_g.reshape(1, -1))
    return y, jnp.swapaxes(cn, 0, 1), so.reshape(state_ssm.shape)


def _peer_query_kernel(h_ref, g_ref, wqt_ref, xnt_ref, qt_ref, xnt_scr):
    j = pl.program_id(1)

    @pl.when(j == 0)
    def _():
        x = h_ref[...]
        ms = jnp.mean(x * x, axis=-1, keepdims=True)
        xn = x * lax.rsqrt(ms + NORM_EPS) * g_ref[...]
        xnt = xn.T.astype(BF16)
        xnt_scr[...] = xnt
        xnt_ref[...] = xnt

    qt_ref[...] = _dot(wqt_ref[...], xnt_scr[...])


def _peer_query(h, g, wqt16, *, tm, tr):
    m, d = h.shape
    nq = wqt16.shape[0]
    return pl.pallas_call(
        _peer_query_kernel,
        grid=(m // tm, nq // tr),
        in_specs=[pl.BlockSpec((tm, d), lambda i, j: (i, 0)),
                  pl.BlockSpec((1, d), lambda i, j: (0, 0)),
                  pl.BlockSpec((tr, d), lambda i, j: (j, 0))],
        out_specs=[pl.BlockSpec((d, tm), lambda i, j: (0, i)),
                   pl.BlockSpec((tr, tm), lambda i, j: (j, i))],
        out_shape=[jax.ShapeDtypeStruct((d, m), BF16),
                   jax.ShapeDtypeStruct((nq, m), F32)],
        scratch_shapes=[pltpu.VMEM((d, tm), BF16)],
        compiler_params=_params(("parallel", "arbitrary")),
        name="peer_query",
    )(h, g, wqt16)


def _pair_candidates():
    keep = [(ka, PEER_TOPK // (ka + 1)) for ka in range(PEER_TOPK)]
    pos = [ka * PEER_TOPK + kb for ka, nb in keep for kb in range(nb)]
    rows = -(-len(pos) // 8) * 8
    return keep, pos, rows


def _peer_retrieve_kernel(qt_ref, keys_ref, pos_ref, s1_ref, c1_ref, thr1_ref, eq1_ref, s2_ref, e2_ref,
                          thr2_ref):
    tb = qt_ref.shape[1]
    nkeys = PEER_KEYS
    iota_n = lax.broadcasted_iota(jnp.int32, (nkeys, tb), 0)
    keep, pos_list, cand_rows = _pair_candidates()
    npair = PEER_TOPK * PEER_TOPK
    iota_p = pos_ref[...]
    for h in range(PEER_HEADS):
        tops, scores, ranks = [], [], []
        for j in range(2):
            hj = 2 * h + j
            s = _dot(keys_ref[hj], qt_ref[hj * PEER_HALF:(hj + 1) * PEER_HALF, :].astype(BF16))
            cur = s
            rank = jnp.full((nkeys, tb), float(PEER_TOPK), F32)
            vals = []
            for k in range(PEER_TOPK):
                m = jnp.max(cur, axis=0, keepdims=True)
                first = jnp.min(jnp.where(cur == m, iota_n, nkeys), axis=0, keepdims=True)
                hit = iota_n == first
                rank = jnp.where(hit, float(k), rank)
                cur = jnp.where(hit, -jnp.inf, cur)
                vals.append(m)
            tops.append(vals)
            scores.append(s)
            ranks.append(rank)
        v2 = jnp.concatenate(tops[1], axis=0)
        pieces = [tops[0][ka] + v2[0:nb] for ka, nb in keep]
        if cand_rows > len(pos_list):
            pieces.append(jnp.full((cand_rows - len(pos_list), tb), -jnp.inf, F32))
        cand = jnp.concatenate(pieces, axis=0)
        picked = []
        pos = None
        for k in range(PEER_TOPK):
            m = jnp.max(cand, axis=0, keepdims=True)
            pos = jnp.min(jnp.where(cand == m, iota_p, npair), axis=0, keepdims=True)
            cand = jnp.where(iota_p == pos, -jnp.inf, cand)
            picked.append(m)
        tau = picked[-1]
        tau_up = jnp.full_like(tau, jnp.inf)
        z = jnp.zeros_like(tau)
        for m in picked:
            tau_up = jnp.minimum(tau_up, jnp.where(m > tau, m, jnp.inf))
            z = z + jnp.exp(m - picked[0])
        posf = pos.astype(F32)
        ra = jnp.floor(posf / PEER_TOPK)
        rb = posf - ra * PEER_TOPK
        m1 = tops[0][0]
        m2 = tops[1][0]
        zsum = z * jnp.exp(picked[0] - (m1 + m2))
        s1_ref[h, 0] = scores[0]
        c1_ref[h, 0] = jnp.exp(scores[0] - m1) / zsum
        thr1_ref[h, 0] = jnp.where(ranks[0] < ra, tau, tau_up)
        eq1_ref[h, 0] = jnp.where(ranks[0] == ra, 1.0, 0.0)
        s2_ref[h, 0] = scores[1]
        e2_ref[h, 0] = jnp.exp(scores[1] - m2)
        thr2_ref[h, 0] = jnp.where(ranks[1] <= rb, tau, tau_up)


def _peer_retrieve(qt, keys16):
    m = qt.shape[1]
    tb = LANES
    nhj = 2 * PEER_HEADS
    spec = pl.BlockSpec((PEER_HEADS, 1, PEER_KEYS, tb), lambda i: (0, i, 0, 0))
    shape = jax.ShapeDtypeStruct((PEER_HEADS, m // tb, PEER_KEYS, tb), F32)
    _, pos_list, cand_rows = _pair_candidates()
    pos = np.full((cand_rows, tb), PEER_TOPK * PEER_TOPK, np.int32)
    pos[:len(pos_list)] = np.asarray(pos_list, np.int32)[:, None]
    return pl.pallas_call(
        _peer_retrieve_kernel,
        grid=(m // tb,),
        in_specs=[pl.BlockSpec((qt.shape[0], tb), lambda i: (0, i)),
                  pl.BlockSpec((nhj, PEER_KEYS, PEER_HALF), lambda i: (0, 0, 0)),
                  pl.BlockSpec((cand_rows, tb), lambda i: (0, 0))],
        out_specs=[spec] * 7,
        out_shape=[shape] * 7,
        compiler_params=_params(("parallel",)),
        name="peer_retrieve",
    )(qt, keys16, jnp.asarray(pos))


def _gelu(x):
    return 0.5 * x * (1.0 + lax.erf(x * (2.0 ** -0.5)))


def _peer_expert_kernel(xnt_ref, s1_ref, c1_ref, thr1_ref, eq1_ref, s2_ref, e2_ref, thr2_ref,
                        u_ref, vt_ref, o_ref, act0, act1, hd, *, n_eblk, a_per_blk):
    e = pl.program_id(1)
    nk = PEER_KEYS
    tt = xnt_ref.shape[1]

    def u_dot(dst):
        dst[...] = _dot(u_ref[...], xnt_ref[...])

    def mask_gelu(src, ai):
        a = (e - 1) * a_per_blk + ai
        row = pl.ds(a, 1)
        for tl in range(tt // LANES):
            ls = slice(tl * LANES, (tl + 1) * LANES)
            w = jnp.zeros((nk, LANES), F32)
            for h in range(PEER_HEADS):
                thr = jnp.where(eq1_ref[h, tl, row, :] > 0.5, thr2_ref[h, tl], thr1_ref[h, tl, row, :])
                pair = s2_ref[h, tl] + s1_ref[h, tl, row, :]
                w = w + jnp.where(pair >= thr, e2_ref[h, tl] * c1_ref[h, tl, row, :], 0.0)
            hd[ai * nk:(ai + 1) * nk, ls] = (_gelu(src[ai * nk:(ai + 1) * nk, ls]) * w).astype(BF16)

    def skewed(dst, src):
        u_dot(dst)
        for ai in range(a_per_blk):
            mask_gelu(src, ai)
        o_ref[...] += _dot(vt_ref[...], hd[...])

    @pl.when(e == 0)
    def _():
        o_ref[...] = jnp.zeros(o_ref.shape, F32)
        u_dot(act0)

    @pl.when((e > 0) & (e < n_eblk) & (e % 2 == 1))
    def _():
        skewed(act1, act0)

    @pl.when((e > 0) & (e < n_eblk) & (e % 2 == 0))
    def _():
        skewed(act0, act1)

    @pl.when(e == n_eblk)
    def _():
        src = act1 if (n_eblk - 1) % 2 == 1 else act0
        for ai in range(a_per_blk):
            mask_gelu(src, ai)
        o_ref[...] += _dot(vt_ref[...], hd[...])


def _peer_experts(xnt, tiles, u16, vt16, *, tt, eb):
    d, m = xnt.shape
    n_exp = u16.shape[0]
    n_eblk = n_exp // eb
    once = pl.Buffered(1)
    tile_spec = pl.BlockSpec((PEER_HEADS, tt // LANES, PEER_KEYS, LANES), lambda i, e: (0, i, 0, 0),
                             pipeline_mode=once)
    return pl.pallas_call(
        functools.partial(_peer_expert_kernel, n_eblk=n_eblk, a_per_blk=eb // PEER_KEYS),
        grid=(m // tt, n_eblk + 1),
        in_specs=[pl.BlockSpec((d, tt), lambda i, e: (0, i), pipeline_mode=once)]
        + [tile_spec] * 7
        + [pl.BlockSpec((eb, d), lambda i, e: (jnp.minimum(e, n_eblk - 1), 0)),
           pl.BlockSpec((d, eb), lambda i, e: (0, jnp.maximum(e - 1, 0)))],
        out_specs=pl.BlockSpec((d, tt), lambda i, e: (0, i), pipeline_mode=once),
        out_shape=jax.ShapeDtypeStruct((d, m), F32),
        scratch_shapes=[pltpu.VMEM((eb, tt), F32), pltpu.VMEM((eb, tt), F32), pltpu.VMEM((eb, tt), BF16)],
        compiler_params=_params(("parallel", "arbitrary")),
        name="peer_experts",
    )(xnt, *tiles, u16, vt16)


def _final_kernel(h_ref, ft_ref, g_ref, o_ref):
    x = h_ref[...] + ft_ref[...].T
    ms = jnp.mean(x * x, axis=-1, keepdims=True)
    o_ref[...] = x * lax.rsqrt(ms + NORM_EPS) * g_ref[...]


def _final_norm(h, ffn_t, g, *, tm):
    m, d = h.shape
    row = pl.BlockSpec((tm, d), lambda i: (i, 0))
    return pl.pallas_call(
        _final_kernel,
        grid=(m // tm,),
        in_specs=[row, pl.BlockSpec((d, tm), lambda i: (0, i)), pl.BlockSpec((1, d), lambda i: (0, 0))],
        out_specs=row,
        out_shape=jax.ShapeDtypeStruct((m, d), F32),
        compiler_params=_params(("parallel",)),
        name="final_norm",
    )(h, ffn_t, g)


def _tile(m, pref):
    return pref if m % pref == 0 else m


def _in_proj(x, g_mix, w16, wdt16, *, transposed):
    m = x.shape[0]
    tm = _tile(m, 512)
    qw = 2 * ATTN_HEADS * ATTN_HEAD_DIM
    t16 = "bf16T" if transposed else "bf16"
    segs = [(0, qw, ATTN_SCALE, (t16,)), (qw, qw, 1.0, ("f32", "bf16")), (2 * qw, ATTN_WIDTH, 1.0, ("f32", t16)),
            (2 * qw + ATTN_WIDTH, SSM_WIDTH, 1.0, ("f32",)),
            (2 * qw + ATTN_WIDTH + SSM_WIDTH, SSM_CONV_DIM, 1.0, ("f32",))]
    q16, k32, k16, v32, v16, z32, xbc32, xn16 = _norm_mm(x, g_mix, w16, segs, tm=tm, tn=512, emit_xn=True)
    dt_raw = _mm([xn16], [wdt16], None, tm=tm, tn=LANES)
    return q16, k32, k16, v32, v16, z32, xbc32, dt_raw


def _peer_and_final(h, g_ffn, wqt16, keys16, u16, vt16, g_final):
    m = h.shape[0]
    xnt, qt = _peer_query(h, g_ffn, wqt16, tm=_tile(m, 256), tr=512)
    tiles = _peer_retrieve(qt, keys16)
    ffn_t = _peer_experts(xnt, tiles, u16, vt16, tt=_tile(m, 512), eb=512)
    return _final_norm(h, ffn_t, g_final, tm=_tile(m, 256))


def kernel(x_prompt, x_sample, cache_k, cache_v, state_conv, state_ssm, page_table, g_mix, w_in, lambda_q, lambda_k, subln_g, conv_w, conv_b, dt_bias, a_log, d_skip, ssm_norm_g, w_out, g_ffn, peer_wq, peer_keys, peer_u, peer_v, g_final):
    assert w_in.shape[0] == 1, "single trunk layer"
    bsz, seq, dm = x_prompt.shape
    nb = x_sample.shape[0]
    assert bsz == 1 and x_sample.shape[1] == 1
    main_cols = w_in.shape[2] - SSM_HEADS
    w16 = w_in[0].astype(BF16)
    wdt16 = jnp.pad(w_in[0][:, main_cols:], ((0, 0), (0, LANES - SSM_HEADS))).astype(BF16)
    wo16 = w_out[0].astype(BF16)
    wqt16 = peer_wq[0].T.astype(BF16)
    keys16 = peer_keys[0].reshape(2 * PEER_HEADS, PEER_KEYS, PEER_HALF).astype(BF16)
    u16 = peer_u[0].astype(BF16)
    vt16 = peer_v[0].T.astype(BF16)
    gm = g_mix[0].reshape(1, -1)
    gf = g_ffn[0].reshape(1, -1)
    gfin = g_final.reshape(1, -1)
    sg = subln_g[0].reshape(1, -1)
    lq, lk = lambda_q[0], lambda_k[0]

    xp = x_prompt.reshape(seq, dm)
    qt16, k32, k16, v32, vtok16, z32, xbc32, dtr = _in_proj(xp, gm, w16, wdt16, transposed=True)
    attn16 = _prompt_attention(qt16, k16, vtok16, lq, lk, sg, t=_tile(seq, 512)).T
    y16, hfin = _ssd_prompt(xbc32, z32, dtr, conv_w[0], conv_b[0], dt_bias[0], a_log[0], d_skip[0],
                            ssm_norm_g[0])
    hp = _mm([attn16, y16], [wo16[:ATTN_WIDTH], wo16[ATTN_WIDTH:]], xp, tm=_tile(seq, 512), tn=512)
    y_prompt = _peer_and_final(hp, gf, wqt16, keys16, u16, vt16, gfin)

    xs = x_sample.reshape(nb, dm)
    sq16, sk32, _, sv32, _, sz32, sxbc32, sdtr = _in_proj(xs, gm, w16, wdt16, transposed=False)
    sattn = _sample_attention(sq16.astype(F32), sk32, sv32, cache_k[0], cache_v[0], page_table, lq, lk, sg)
    sy, conv_s, ssm_s = _ssd_sample(sxbc32, sz32, sdtr, state_conv[0], state_ssm[0], conv_w[0], conv_b[0],
                                    dt_bias[0], a_log[0], d_skip[0], ssm_norm_g[0])
    hs = _mm([sattn.astype(BF16), sy.astype(BF16)], [wo16[:ATTN_WIDTH], wo16[ATTN_WIDTH:]], xs,
             tm=nb, tn=512)
    y_sample = _peer_and_final(hs, gf, wqt16, keys16, u16, vt16, gfin)

    return (y_prompt.reshape(bsz, seq, dm),
            y_sample.reshape(nb, 1, dm),
            k32.reshape(1, bsz, seq, ATTN_HEADS, 2, ATTN_HEAD_DIM),
            v32.reshape(1, bsz, seq, ATTN_HEADS, ATTN_V_DIM),
            xbc32[seq - (SSM_CONV - 1):].reshape(1, bsz, SSM_CONV - 1, SSM_CONV_DIM),
            hfin.reshape(1, bsz, SSM_HEADS, SSM_HEAD_DIM, SSM_STATE),
            sk32.reshape(1, nb, 1, ATTN_HEADS, 2, ATTN_HEAD_DIM),
            sv32.reshape(1, nb, 1, ATTN_HEADS, ATTN_V_DIM),
            conv_s.reshape(1, nb, SSM_CONV - 1, SSM_CONV_DIM),
            ssm_s.reshape(1, nb, SSM_HEADS, SSM_HEAD_DIM, SSM_STATE))
```

```python
import functools
import math

import jax
import jax.numpy as jnp
import numpy as np
from jax import lax
from jax.experimental import pallas as pl
from jax.experimental.pallas import tpu as pltpu

F32 = jnp.float32
BF16 = jnp.bfloat16

ATTN_HEADS = 8
ATTN_HEAD_DIM = 128
ATTN_V_DIM = 256
ATTN_WIDTH = ATTN_HEADS * ATTN_V_DIM
SSM_WIDTH = 2048
SSM_HEAD_DIM = 64
SSM_HEADS = 32
SSM_GROUPS = 8
SSM_STATE = 128
SSM_CONV = 4
SSM_CHUNK = 128
SSM_CONV_DIM = SSM_WIDTH + 2 * SSM_GROUPS * SSM_STATE
GROUP_WIDTH = SSM_WIDTH // SSM_GROUPS
PEER_HEADS = 8
PEER_KEYS = 128
PEER_HALF = 128
PEER_TOPK = 16
NORM_EPS = 1e-6
LAM_INIT = 0.8 - 0.6 * math.exp(-0.3 * 0)
ATTN_SCALE = ATTN_HEAD_DIM ** -0.5
LANES = 128
VMEM_LIMIT = 56 * 1024 * 1024


def _params(sem):
    return pltpu.CompilerParams(dimension_semantics=sem, vmem_limit_bytes=VMEM_LIMIT)


def _split3(x):
    x1 = x.astype(BF16)
    r1 = x - x1.astype(F32)
    x2 = r1.astype(BF16)
    x3 = (r1 - x2.astype(F32)).astype(BF16)
    return x1, x2, x3


def _dot(a, b):
    return jnp.dot(a, b, preferred_element_type=F32)


def _dot_nt(a, b):
    return lax.dot_general(a, b, (((1,), (1,)), ((), ())), preferred_element_type=F32)


def _dot_exact_rhs(x, w16):
    x1, x2, x3 = _split3(x)
    return _dot(x1, w16) + _dot(x2, w16) + _dot(x3, w16)


def _dot_exact_lhs(w16, x):
    x1, x2, x3 = _split3(x)
    return _dot(w16, x1) + _dot(w16, x2) + _dot(w16, x3)


def _silu(x):
    return x / (1.0 + jnp.exp(-x))


def _softplus(x):
    return jnp.maximum(x, 0.0) + jnp.log1p(jnp.exp(-jnp.abs(x)))


def _lambda(lq_ref, lk_ref):
    e = jnp.exp(jnp.sum(lq_ref[...] * lk_ref[...], axis=-1, keepdims=True))
    return e[0:1, :] - e[1:2, :] + LAM_INIT


def _alibi_slopes():
    return 2.0 ** (-8.0 * np.arange(1, ATTN_HEADS + 1) / ATTN_HEADS)


def _norm_mm_kernel(x_ref, g_ref, w_ref, *rest, segs, emit_xn):
    outs = list(rest[:-1])
    xn_scr = rest[-1]
    j = pl.program_id(1)

    @pl.when(j == 0)
    def _():
        x = x_ref[...]
        ms = jnp.mean(x * x, axis=-1, keepdims=True)
        xn = (x * lax.rsqrt(ms + NORM_EPS) * g_ref[...]).astype(BF16)
        xn_scr[...] = xn
        if emit_xn:
            outs[-1][...] = xn

    acc = _dot(xn_scr[...], w_ref[...])
    k = 0
    for (start, nblk, scale, kinds) in segs:
        refs = [(outs[k + i], kind) for i, kind in enumerate(kinds)]
        k += len(kinds)

        @pl.when((j >= start) & (j < start + nblk))
        def _(refs=refs, scale=scale):
            val = acc if scale == 1.0 else acc * scale
            for r, kind in refs:
                r[...] = (val.T if kind.endswith("T") else val).astype(r.dtype)


def _norm_mm(x, g, w16, segs, *, tm, tn, emit_xn):
    m, kdim = x.shape
    nblocks = sum(wd for _, wd, _, _ in segs) // tn
    bsegs = [(s // tn, wd // tn, scale, kinds) for s, wd, scale, kinds in segs]
    out_shapes, out_specs = [], []
    for (sb, nb, _, kinds), (_, wd, _, _) in zip(bsegs, segs):
        def imap(i, j, sb=sb, nb=nb):
            return (i, jnp.clip(j - sb, 0, nb - 1))

        def imap_t(i, j, sb=sb, nb=nb):
            return (jnp.clip(j - sb, 0, nb - 1), i)
        for kind in kinds:
            dt = F32 if kind == "f32" else BF16
            if kind.endswith("T"):
                out_shapes.append(jax.ShapeDtypeStruct((wd, m), dt))
                out_specs.append(pl.BlockSpec((tn, tm), imap_t))
            else:
                out_shapes.append(jax.ShapeDtypeStruct((m, wd), dt))
                out_specs.append(pl.BlockSpec((tm, tn), imap))
    if emit_xn:
        out_shapes.append(jax.ShapeDtypeStruct((m, kdim), BF16))
        out_specs.append(pl.BlockSpec((tm, kdim), lambda i, j: (i, 0)))
    return pl.pallas_call(
        functools.partial(_norm_mm_kernel, segs=bsegs, emit_xn=emit_xn),
        grid=(m // tm, nblocks),
        in_specs=[pl.BlockSpec((tm, kdim), lambda i, j: (i, 0), pipeline_mode=pl.Buffered(1)),
                  pl.BlockSpec((1, kdim), lambda i, j: (0, 0)),
                  pl.BlockSpec((kdim, tn), lambda i, j: (0, j))],
        out_specs=out_specs,
        out_shape=out_shapes,
        scratch_shapes=[pltpu.VMEM((tm, kdim), BF16)],
        compiler_params=_params(("parallel", "arbitrary")),
        name="norm_mm",
    )(x, g, w16)


def _mm_kernel(*refs, n_pairs, has_res):
    o_ref = refs[-1]
    acc = _dot(refs[0][...], refs[n_pairs][...])
    for i in range(1, n_pairs):
        acc = acc + _dot(refs[i][...], refs[n_pairs + i][...])
    if has_res:
        acc = acc + refs[2 * n_pairs][...]
    o_ref[...] = acc


def _mm(a_list, w_list, res, *, tm, tn):
    m = a_list[0].shape[0]
    n = w_list[0].shape[1]
    in_specs = [pl.BlockSpec((tm, a.shape[1]), lambda i, j: (i, 0)) for a in a_list]
    in_specs += [pl.BlockSpec((w.shape[0], tn), lambda i, j: (0, j)) for w in w_list]
    args = list(a_list) + list(w_list)
    if res is not None:
        in_specs.append(pl.BlockSpec((tm, tn), lambda i, j: (i, j)))
        args.append(res)
    return pl.pallas_call(
        functools.partial(_mm_kernel, n_pairs=len(a_list), has_res=res is not None),
        grid=(m // tm, n // tn),
        in_specs=in_specs,
        out_specs=pl.BlockSpec((tm, tn), lambda i, j: (i, j)),
        out_shape=jax.ShapeDtypeStruct((m, n), F32),
        compiler_params=_params(("parallel", "parallel")),
        name="mm",
    )(*args)


def _attn_kernel(qi_tab, kj_tab, slopes_ref, qt_ref, k_ref, vt_ref, lq_ref, lk_ref, g_ref, o_ref,
                 m_scr, l_scr, acc_scr, *, t):
    h = pl.program_id(0)
    step = pl.program_id(1)
    qi = qi_tab[step]
    kj = kj_tab[step]
    nrep = t // LANES

    @pl.when(kj == 0)
    def _():
        m_scr[...] = jnp.full(m_scr.shape, -jnp.inf, F32)
        l_scr[...] = jnp.zeros(l_scr.shape, F32)
        acc_scr[...] = jnp.zeros(acc_scr.shape, F32)

    def update(masked):
        krow = lax.broadcasted_iota(jnp.int32, (t, LANES), 0)
        kbias = slopes_ref[h] * (krow + (kj - qi) * t).astype(F32)
        bias = jnp.concatenate([kbias] * nrep, axis=1)
        qt = qt_ref[...]
        k = k_ref[...]
        vt = vt_ref[...]
        if masked:
            causal = (lax.broadcasted_iota(jnp.int32, (t, t), 1)
                      >= lax.broadcasted_iota(jnp.int32, (t, t), 0))
        ss = [_dot(k[:, j * 128:(j + 1) * 128], qt[j * 128:(j + 1) * 128, :]) + bias for j in range(2)]
        if masked:
            ss = [jnp.where(causal, s, -jnp.inf) for s in ss]
        ps = []
        for j in range(2):
            s = ss[j]
            m_prev = m_scr[j]
            m_new = jnp.maximum(m_prev, jnp.max(s, axis=0, keepdims=True))
            alpha = jnp.exp(m_prev - m_new)
            p = jnp.exp(s - m_new)
            l_scr[j] = alpha * l_scr[j] + jnp.sum(p, axis=0, keepdims=True)
            acc_scr[j] = alpha * acc_scr[j]
            m_scr[j] = m_new
            ps.append(p.astype(BF16))
        pv = _dot(vt, jnp.concatenate(ps, axis=1))
        for j in range(2):
            acc_scr[j] += pv[:, j * t:(j + 1) * t]

    @pl.when(kj < qi)
    def _():
        update(False)

    @pl.when(kj == qi)
    def _():
        update(True)
        lam = _lambda(lq_ref, lk_ref)
        d = acc_scr[0] / l_scr[0] - lam * (acc_scr[1] / l_scr[1])
        ms = jnp.mean(d * d, axis=0, keepdims=True)
        g = jnp.concatenate([g_ref[...]] * nrep, axis=1)
        o_ref[...] = (d * lax.rsqrt(ms + NORM_EPS) * g * (1.0 - LAM_INIT)).astype(o_ref.dtype)


def _prompt_attention(qt16, k16, vt16, lam_q, lam_k, subln_g, *, t):
    l = k16.shape[0]
    nq = l // t
    pairs = [(qi, kj) for qi in range(nq) for kj in range(qi + 1)]
    qi_tab = jnp.asarray([p[0] for p in pairs], jnp.int32)
    kj_tab = jnp.asarray([p[1] for p in pairs], jnp.int32)
    slopes = jnp.asarray(_alibi_slopes(), F32)
    g_tile = jnp.broadcast_to(subln_g.reshape(ATTN_V_DIM, 1), (ATTN_V_DIM, LANES))

    def full(shape):
        return pl.BlockSpec(shape, lambda h, s, qt, kt: tuple(0 for _ in shape))

    return pl.pallas_call(
        functools.partial(_attn_kernel, t=t),
        grid_spec=pltpu.PrefetchScalarGridSpec(
            num_scalar_prefetch=2,
            grid=(ATTN_HEADS, len(pairs)),
            in_specs=[pl.BlockSpec(memory_space=pltpu.SMEM),
                      pl.BlockSpec((256, t), lambda h, s, qt, kt: (h, qt[s])),
                      pl.BlockSpec((t, 256), lambda h, s, qt, kt: (kt[s], h)),
                      pl.BlockSpec((256, t), lambda h, s, qt, kt: (h, kt[s])),
                      full((2, 128)), full((2, 128)), full((ATTN_V_DIM, LANES))],
            out_specs=pl.BlockSpec((256, t), lambda h, s, qt, kt: (h, qt[s])),
            scratch_shapes=[pltpu.VMEM((2, 1, t), F32),
                            pltpu.VMEM((2, 1, t), F32),
                            pltpu.VMEM((2, ATTN_V_DIM, t), F32)]),
        out_shape=jax.ShapeDtypeStruct((ATTN_WIDTH, l), BF16),
        compiler_params=_params(("parallel", "arbitrary")),
        name="prompt_attn",
    )(qi_tab, kj_tab, slopes, qt16, k16, vt16, lam_q, lam_k, g_tile)


def _rows_from_lanes(row, n, width):
    return jnp.concatenate([row[:, i * width:(i + 1) * width] for i in range(n)], axis=0)


def _map_major_rows(row):
    w = ATTN_HEAD_DIM
    return jnp.concatenate([row[:, (h * 2 + j) * w:(h * 2 + j + 1) * w]
                            for j in range(2) for h in range(ATTN_HEADS)], axis=0)


def _sample_attn_kernel(pt_ref, q_ref, kn_ref, vn_ref, *rest, n_steps, pages_per_step, page, past):
    del pt_ref
    g_pages = pages_per_step
    k_refs = rest[:g_pages]
    v_refs = rest[g_pages:2 * g_pages]
    slope_ref, lq_ref, lk_ref, g_ref, o_ref, q_scr, m_scr, l_scr, acc_scr = rest[2 * g_pages:]
    p = pl.program_id(1)
    nh = ATTN_HEADS
    rows = page * nh

    @pl.when(p == 0)
    def _():
        q_scr[...] = _map_major_rows(q_ref[0]).astype(BF16)
        m_scr[...] = jnp.full(m_scr.shape, -jnp.inf, F32)
        l_scr[...] = jnp.zeros(l_scr.shape, F32)
        acc_scr[...] = jnp.zeros(acc_scr.shape, F32)

    lane = lax.broadcasted_iota(jnp.int32, (nh, rows), 1)
    own = lane % nh == lax.broadcasted_iota(jnp.int32, (nh, rows), 0)
    slope = slope_ref[...][:, :1]
    scores = []
    for gi in range(g_pages):
        kpos = (p * g_pages + gi) * page + lane // nh
        bias = slope * (past - kpos).astype(F32)
        sj = []
        for j in range(2):
            kj = k_refs[gi][pl.ds(j, rows, stride=2), :].astype(BF16)
            s = _dot_nt(q_scr[j * nh:(j + 1) * nh, :], kj) - bias
            sj.append(jnp.where(own, s, -jnp.inf))
        scores.append(jnp.concatenate(sj, axis=0))
    m_prev = m_scr[...]
    m_new = m_prev
    for s in scores:
        m_new = jnp.maximum(m_new, jnp.max(s, axis=-1, keepdims=True))
    alpha = jnp.exp(m_prev - m_new)
    l_new = alpha * l_scr[...]
    acc = alpha[:, :1] * acc_scr[...]
    for gi, s in enumerate(scores):
        pe = jnp.exp(s - m_new[:, :1])
        l_new = l_new + jnp.sum(pe, axis=-1, keepdims=True)
        acc = acc + _dot(pe.astype(BF16), v_refs[gi][...].astype(BF16))
    m_scr[...] = m_new
    l_scr[...] = l_new
    acc_scr[...] = acc

    @pl.when(p == n_steps - 1)
    def _():
        kn = _map_major_rows(kn_ref[0]).astype(BF16).astype(F32)
        vn = _rows_from_lanes(vn_ref[0], nh, ATTN_V_DIM).astype(BF16).astype(F32)
        vn2 = jnp.concatenate([vn, vn], axis=0)
        s_self = jnp.sum(q_scr[...].astype(F32) * kn, axis=-1, keepdims=True)
        m_prev = m_scr[...]
        m_new = jnp.maximum(m_prev, s_self)
        alpha = jnp.exp(m_prev - m_new)
        p_self = jnp.exp(s_self - m_new)
        l_fin = alpha * l_scr[...] + p_self
        acc = alpha[:, :1] * acc_scr[...] + p_self[:, :1] * vn2
        n = acc / l_fin[:, :1]
        d = n[0:nh, :] - _lambda(lq_ref, lk_ref) * n[nh:2 * nh, :]
        ms = jnp.mean(d * d, axis=-1, keepdims=True)
        o_ref[0] = d * lax.rsqrt(ms + NORM_EPS) * g_ref[...] * (1.0 - LAM_INIT)


def _sample_attention(q, k_new, v_new, cache_k, cache_v, page_table, lam_q, lam_k, subln_g):
    b = q.shape[0]
    page = cache_k.shape[1]
    n_pages = page_table.shape[1]
    width = q.shape[1]
    ck = cache_k.reshape(-1, ATTN_HEAD_DIM)
    cv = cache_v.reshape(-1, ATTN_V_DIM)
    g_pages = math.gcd(n_pages, 8)
    n_steps = n_pages // g_pages
    slope_tile = jnp.asarray(np.repeat(_alibi_slopes()[:, None], LANES, axis=1), F32)
    row_spec = pl.BlockSpec((1, 1, width), lambda i, p, pt: (i, 0, 0))

    def page_spec(rows, lanes, gi):
        return pl.BlockSpec((rows, lanes), lambda i, p, pt: (pt[i * n_pages + p * g_pages + gi], 0))

    def full(shape):
        return pl.BlockSpec(shape, lambda i, p, pt: tuple(0 for _ in shape))

    out = pl.pallas_call(
        functools.partial(_sample_attn_kernel, n_steps=n_steps, pages_per_step=g_pages, page=page,
                          past=n_pages * page),
        grid_spec=pltpu.PrefetchScalarGridSpec(
            num_scalar_prefetch=1,
            grid=(b, n_steps),
            in_specs=[row_spec, row_spec, row_spec]
            + [page_spec(page * ATTN_HEADS * 2, ATTN_HEAD_DIM, gi) for gi in range(g_pages)]
            + [page_spec(page * ATTN_HEADS, ATTN_V_DIM, gi) for gi in range(g_pages)]
            + [full((ATTN_HEADS, LANES)), full((2, 128)), full((2, 128)), full((1, ATTN_V_DIM))],
            out_specs=pl.BlockSpec((1, ATTN_HEADS, ATTN_V_DIM), lambda i, p, pt: (i, 0, 0)),
            scratch_shapes=[pltpu.VMEM((2 * ATTN_HEADS, ATTN_HEAD_DIM), BF16),
                            pltpu.VMEM((2 * ATTN_HEADS, LANES), F32),
                            pltpu.VMEM((2 * ATTN_HEADS, LANES), F32),
                            pltpu.VMEM((2 * ATTN_HEADS, ATTN_V_DIM), F32)]),
        out_shape=jax.ShapeDtypeStruct((b, ATTN_HEADS, ATTN_V_DIM), F32),
        compiler_params=_params(("parallel", "arbitrary")),
        name="sample_attn",
    )(page_table.reshape(-1), q.reshape(b, 1, width), k_new.reshape(b, 1, width),
      v_new.reshape(b, 1, width), *([ck] * g_pages), *([cv] * g_pages), slope_tile, lam_q, lam_k, subln_g)
    return out.reshape(b, ATTN_WIDTH)


def _ssd_prompt_kernel(xbc_ref, z_ref, dtr_ref, cw_ref, cb_ref, dtb_ref, alog_ref, dexp_ref, ng_ref,
                       e_ref, y_ref, hfin_ref, buf, xc, ht, *, n_chunks):
    c = pl.program_id(0)
    q = SSM_CHUNK

    @pl.when(c == 0)
    def _():
        buf[0:8, :] = jnp.zeros((8, SSM_CONV_DIM), F32)
        ht[...] = jnp.zeros(ht.shape, F32)

    buf[8:8 + q, :] = xbc_ref[...]
    blk = 512
    for cbk in range(SSM_CONV_DIM // blk):
        sl = slice(cbk * blk, (cbk + 1) * blk)
        acc = cb_ref[:, sl]
        for j in range(SSM_CONV):
            acc = acc + cw_ref[j:j + 1, sl] * buf[5 + j:5 + j + q, sl]
        xc[:, sl] = _silu(acc)
    buf[0:8, :] = buf[q:q + 8, :]

    dt = _softplus(dtr_ref[...] + dtb_ref[...])
    a = dt * (-jnp.exp(alog_ref[...]))
    ri = lax.broadcasted_iota(jnp.int32, (q, q), 0)
    ci = lax.broadcasted_iota(jnp.int32, (q, q), 1)
    causal = ri >= ci
    tril = jnp.where(causal, 1.0, 0.0).astype(BF16)
    a_cum = _dot_exact_lhs(tril, a)
    a_cum_t = a_cum.T
    e16 = e_ref[...]
    ac_exp = _dot_exact_rhs(a_cum, e16)
    dt_exp = _dot_exact_rhs(dt, e16)
    ac_last = ac_exp[q - 1:q, :]
    xs = xc[:, 0:SSM_WIDTH]
    xdt = xs * dt_exp
    xw = xdt * jnp.exp(ac_last - ac_exp)
    ea = jnp.exp(ac_exp)
    cdec = jnp.exp(ac_last)
    lane_g = lax.broadcasted_iota(jnp.int32, (q, GROUP_WIDTH), 1)

    for g in range(SSM_GROUPS):
        gs = slice(g * GROUP_WIDTH, (g + 1) * GROUP_WIDTH)
        bm = xc[:, SSM_WIDTH + g * SSM_STATE:SSM_WIDTH + (g + 1) * SSM_STATE]
        cm = xc[:, SSM_WIDTH + (SSM_GROUPS + g) * SSM_STATE:SSM_WIDTH + (SSM_GROUPS + g + 1) * SSM_STATE]
        bm16 = bm.astype(BF16)
        cm16 = cm.astype(BF16)
        cbm = _dot_nt(cm16, bm16)
        xdt16 = xdt[:, gs].astype(BF16)
        yg = jnp.zeros((q, GROUP_WIDTH), F32)
        for r in range(SSM_HEADS // SSM_GROUPS):
            hd = g * (SSM_HEADS // SSM_GROUPS) + r
            seg = a_cum[:, hd:hd + 1] - a_cum_t[hd:hd + 1, :]
            dec = jnp.exp(jnp.where(causal, seg, -jnp.inf))
            yr = _dot((cbm * dec).astype(BF16), xdt16)
            yg = jnp.where(lane_g // SSM_HEAD_DIM == r, yr, yg)
        h_in = ht[:, gs]
        y_off = _dot(cm16, h_in.astype(BF16)) * ea[:, gs]
        st = _dot(bm.T.astype(BF16), xw[:, gs].astype(BF16))
        ht[:, gs] = cdec[:, gs] * h_in + st
        y = yg + y_off + dexp_ref[:, gs] * xs[:, gs]
        gated = y * _silu(z_ref[:, gs])
        ms = jnp.mean(gated * gated, axis=-1, keepdims=True)
        y_ref[:, gs] = (gated * lax.rsqrt(ms + NORM_EPS) * ng_ref[:, gs]).astype(y_ref.dtype)

    @pl.when(c == n_chunks - 1)
    def _():
        hfin_ref[...] = ht[...].T


def _head_expand_matrix():
    e = np.zeros((LANES, SSM_WIDTH), np.float32)
    for hd in range(SSM_HEADS):
        e[hd, hd * SSM_HEAD_DIM:(hd + 1) * SSM_HEAD_DIM] = 1.0
    return jnp.asarray(e, BF16)


def _pad_lanes(v):
    return jnp.pad(v.reshape(1, -1), ((0, 0), (0, LANES - v.size)))


def _ssd_prompt(xbc, z, dt_raw, conv_w, conv_b, dt_bias, a_log, d_skip, norm_g):
    l = xbc.shape[0]
    n_chunks = l // SSM_CHUNK
    q = SSM_CHUNK

    def full(shape):
        return pl.BlockSpec(shape, lambda c: tuple(0 for _ in shape))

    y16, hfin = pl.pallas_call(
        functools.partial(_ssd_prompt_kernel, n_chunks=n_chunks),
        grid=(n_chunks,),
        in_specs=[pl.BlockSpec((q, SSM_CONV_DIM), lambda c: (c, 0)),
                  pl.BlockSpec((q, SSM_WIDTH), lambda c: (c, 0)),
                  pl.BlockSpec((q, LANES), lambda c: (c, 0)),
                  full((SSM_CONV, SSM_CONV_DIM)), full((1, SSM_CONV_DIM)),
                  full((1, LANES)), full((1, LANES)), full((1, SSM_WIDTH)), full((1, SSM_WIDTH)),
                  full((LANES, SSM_WIDTH))],
        out_specs=[pl.BlockSpec((q, SSM_WIDTH), lambda c: (c, 0)),
                   full((SSM_WIDTH, SSM_STATE))],
        out_shape=[jax.ShapeDtypeStruct((l, SSM_WIDTH), BF16),
                   jax.ShapeDtypeStruct((SSM_WIDTH, SSM_STATE), F32)],
        scratch_shapes=[pltpu.VMEM((q + 8, SSM_CONV_DIM), F32),
                        pltpu.VMEM((q, SSM_CONV_DIM), F32),
                        pltpu.VMEM((SSM_STATE, SSM_WIDTH), F32)],
        compiler_params=_params(("arbitrary",)),
        name="ssd_prompt",
    )(xbc, z, dt_raw, conv_w, conv_b.reshape(1, -1), _pad_lanes(dt_bias), _pad_lanes(a_log),
      jnp.repeat(d_skip, SSM_HEAD_DIM).reshape(1, -1), norm_g.reshape(1, -1), _head_expand_matrix())
    return y16, hfin


def _ssd_sample_pre_kernel(xbc_ref, dtr_ref, sc_ref, cw_ref, cb_ref, dtb_ref, aexp_ref, e_ref,
                           xc_ref, cn_ref, xdt_t_ref, dec_t_ref):
    xbc = xbc_ref[...]
    acc = cb_ref[...] + cw_ref[SSM_CONV - 1:SSM_CONV, :] * xbc
    for j in range(SSM_CONV - 1):
        acc = acc + cw_ref[j:j + 1, :] * sc_ref[j]
    xcv = _silu(acc)
    xc_ref[...] = xcv
    cn_ref[0] = sc_ref[1]
    cn_ref[1] = sc_ref[2]
    cn_ref[2] = xbc
    dt = _softplus(dtr_ref[...] + dtb_ref[...])
    dt_exp = _dot_exact_rhs(dt, e_ref[...])
    dec_t_ref[...] = jnp.exp(dt_exp * (-jnp.exp(aexp_ref[...]))).T
    xdt_t_ref[...] = (xcv[:, 0:SSM_WIDTH] * dt_exp).T


def _ssd_sample_state_kernel(xdt_t_ref, dec_t_ref, xc_ref, s_ref, so_ref, yt_ref):
    b = pl.program_id(0)
    nb = xdt_t_ref.shape[1]
    onehot = jnp.where(lax.broadcasted_iota(jnp.int32, (nb, SSM_STATE), 0) == b, 1.0, 0.0).astype(BF16)
    lane = lax.broadcasted_iota(jnp.int32, (GROUP_WIDTH, nb), 1)

    @pl.when(b == 0)
    def _():
        yt_ref[...] = jnp.zeros(yt_ref.shape, F32)

    xc_row = xc_ref[pl.ds(b, 1), :]
    for g in range(SSM_GROUPS):
        gs = slice(g * GROUP_WIDTH, (g + 1) * GROUP_WIDTH)
        xb = _dot_exact_rhs(xdt_t_ref[gs, :], onehot)
        db = _dot_exact_rhs(dec_t_ref[gs, :], onehot)
        bm = xc_row[:, SSM_WIDTH + g * SSM_STATE:SSM_WIDTH + (g + 1) * SSM_STATE]
        cm = xc_row[:, SSM_WIDTH + (SSM_GROUPS + g) * SSM_STATE:SSM_WIDTH + (SSM_GROUPS + g + 1) * SSM_STATE]
        hn = db * s_ref[gs, :] + xb * bm
        so_ref[gs, :] = hn
        ycol = jnp.sum(hn * cm, axis=-1, keepdims=True)
        yt_ref[gs, :] = jnp.where(lane == b, ycol, yt_ref[gs, :])


def _ssd_sample_post_kernel(yt_ref, xc_ref, z_ref, dexp_ref, ng_ref, o_ref):
    yraw = yt_ref[...].T
    for g in range(SSM_GROUPS):
        gs = slice(g * GROUP_WIDTH, (g + 1) * GROUP_WIDTH)
        y = yraw[:, gs] + dexp_ref[:, gs] * xc_ref[:, gs]
        gated = y * _silu(z_ref[:, gs])
        ms = jnp.mean(gated * gated, axis=-1, keepdims=True)
        o_ref[:, gs] = gated * lax.rsqrt(ms + NORM_EPS) * ng_ref[:, gs]


def _ssd_sample(xbc, z, dt_raw, state_conv, state_ssm, conv_w, conv_b, dt_bias, a_log, d_skip, norm_g):
    nb = xbc.shape[0]
    rows = SSM_HEADS * SSM_HEAD_DIM
    sc = jnp.swapaxes(state_conv, 0, 1)
    st = state_ssm.reshape(nb * rows, SSM_STATE)
    xc, cn, xdt_t, dec_t = pl.pallas_call(
        _ssd_sample_pre_kernel,
        out_shape=[jax.ShapeDtypeStruct((nb, SSM_CONV_DIM), F32),
                   jax.ShapeDtypeStruct((SSM_CONV - 1, nb, SSM_CONV_DIM), F32),
                   jax.ShapeDtypeStruct((SSM_WIDTH, nb), F32),
                   jax.ShapeDtypeStruct((SSM_WIDTH, nb), F32)],
        compiler_params=pltpu.CompilerParams(vmem_limit_bytes=VMEM_LIMIT),
        name="ssd_sample_pre",
    )(xbc, dt_raw, sc, conv_w, conv_b.reshape(1, -1), _pad_lanes(dt_bias),
      jnp.repeat(a_log, SSM_HEAD_DIM).reshape(1, -1), _head_expand_matrix())

    def full(shape):
        return pl.BlockSpec(shape, lambda i: tuple(0 for _ in shape))

    so, yt = pl.pallas_call(
        _ssd_sample_state_kernel,
        grid=(nb,),
        in_specs=[full((SSM_WIDTH, nb)), full((SSM_WIDTH, nb)), full((nb, SSM_CONV_DIM)),
                  pl.BlockSpec((rows, SSM_STATE), lambda i: (i, 0))],
        out_specs=[pl.BlockSpec((rows, SSM_STATE), lambda i: (i, 0)), full((SSM_WIDTH, nb))],
        out_shape=[jax.ShapeDtypeStruct(st.shape, F32),
                   jax.ShapeDtypeStruct((SSM_WIDTH, nb), F32)],
        compiler_params=_params(("arbitrary",)),
        name="ssd_sample_state",
    )(xdt_t, dec_t, xc, st)
    y = pl.pallas_call(
        _ssd_sample_post_kernel,
        out_shape=jax.ShapeDtypeStruct((nb, SSM_WIDTH), F32),
        compiler_params=pltpu.CompilerParams(vmem_limit_bytes=VMEM_LIMIT),
        name="ssd_sample_post",
    )(yt, xc, z, jnp.repeat(d_skip, SSM_HEAD_DIM).reshape(1, -1), norm_g.reshape(1, -1))
    return y, jnp.swapaxes(cn, 0, 1), so.reshape(state_ssm.shape)


def _peer_query_kernel(h_ref, g_ref, wqt_ref, xnt_ref, qt_ref, xnt_scr):
    j = pl.program_id(1)

    @pl.when(j == 0)
    def _():
        x = h_ref[...]
        ms = jnp.mean(x * x, axis=-1, keepdims=True)
        xn = x * lax.rsqrt(ms + NORM_EPS) * g_ref[...]
        xnt = xn.T.astype(BF16)
        xnt_scr[...] = xnt
        xnt_ref[...] = xnt

    qt_ref[...] = _dot(wqt_ref[...], xnt_scr[...])


def _peer_query(h, g, wqt16, *, tm, tr):
    m, d = h.shape
    nq = wqt16.shape[0]
    return pl.pallas_call(
        _peer_query_kernel,
        grid=(m // tm, nq // tr),
        in_specs=[pl.BlockSpec((tm, d), lambda i, j: (i, 0)),
                  pl.BlockSpec((1, d), lambda i, j: (0, 0)),
                  pl.BlockSpec((tr, d), lambda i, j: (j, 0))],
        out_specs=[pl.BlockSpec((d, tm), lambda i, j: (0, i)),
                   pl.BlockSpec((tr, tm), lambda i, j: (j, i))],
        out_shape=[jax.ShapeDtypeStruct((d, m), BF16),
                   jax.ShapeDtypeStruct((nq, m), F32)],
        scratch_shapes=[pltpu.VMEM((d, tm), BF16)],
        compiler_params=_params(("parallel", "arbitrary")),
        name="peer_query",
    )(h, g, wqt16)


def _pair_candidates():
    keep = [(ka, PEER_TOPK // (ka + 1)) for ka in range(PEER_TOPK)]
    pos = [ka * PEER_TOPK + kb for ka, nb in keep for kb in range(nb)]
    rows = -(-len(pos) // 8) * 8
    return keep, pos, rows


def _peer_retrieve_kernel(qt_ref, keys_ref, pos_ref, s1_ref, c1_ref, thr1_ref, eq1_ref, s2_ref, e2_ref,
                          thr2_ref):
    tb = qt_ref.shape[1]
    nkeys = PEER_KEYS
    iota_n = lax.broadcasted_iota(jnp.int32, (nkeys, tb), 0)
    keep, pos_list, cand_rows = _pair_candidates()
    npair = PEER_TOPK * PEER_TOPK
    iota_p = pos_ref[...]
    for h in range(PEER_HEADS):
        tops, scores, ranks = [], [], []
        for j in range(2):
            hj = 2 * h + j
            s = _dot(keys_ref[hj], qt_ref[hj * PEER_HALF:(hj + 1) * PEER_HALF, :].astype(BF16))
            cur = s
            rank = jnp.full((nkeys, tb), float(PEER_TOPK), F32)
            vals = []
            for k in range(PEER_TOPK):
                m = jnp.max(cur, axis=0, keepdims=True)
                first = jnp.min(jnp.where(cur == m, iota_n, nkeys), axis=0, keepdims=True)
                hit = iota_n == first
                rank = jnp.where(hit, float(k), rank)
                cur = jnp.where(hit, -jnp.inf, cur)
                vals.append(m)
            tops.append(vals)
            scores.append(s)
            ranks.append(rank)
        v2 = jnp.concatenate(tops[1], axis=0)
        pieces = [tops[0][ka] + v2[0:nb] for ka, nb in keep]
        if cand_rows > len(pos_list):
            pieces.append(jnp.full((cand_rows - len(pos_list), tb), -jnp.inf, F32))
        cand = jnp.concatenate(pieces, axis=0)
        picked = []
        pos = None
        for k in range(PEER_TOPK):
            m = jnp.max(cand, axis=0, keepdims=True)
            pos = jnp.min(jnp.where(cand == m, iota_p, npair), axis=0, keepdims=True)
            cand = jnp.where(iota_p == pos, -jnp.inf, cand)
            picked.append(m)
        tau = picked[-1]
        tau_up = jnp.full_like(tau, jnp.inf)
        z = jnp.zeros_like(tau)
        for m in picked:
            tau_up = jnp.minimum(tau_up, jnp.where(m > tau, m, jnp.inf))
            z = z + jnp.exp(m - picked[0])
        posf = pos.astype(F32)
        ra = jnp.floor(posf / PEER_TOPK)
        rb = posf - ra * PEER_TOPK
        m1 = tops[0][0]
        m2 = tops[1][0]
        zsum = z * jnp.exp(picked[0] - (m1 + m2))
        s1_ref[h, 0] = scores[0]
        c1_ref[h, 0] = jnp.exp(scores[0] - m1) / zsum
        thr1_ref[h, 0] = jnp.where(ranks[0] < ra, tau, tau_up)
        eq1_ref[h, 0] = jnp.where(ranks[0] == ra, 1.0, 0.0)
        s2_ref[h, 0] = scores[1]
        e2_ref[h, 0] = jnp.exp(scores[1] - m2)
        thr2_ref[h, 0] = jnp.where(ranks[1] <= rb, tau, tau_up)


def _peer_retrieve(qt, keys16):
    m = qt.shape[1]
    tb = LANES
    nhj = 2 * PEER_HEADS
    spec = pl.BlockSpec((PEER_HEADS, 1, PEER_KEYS, tb), lambda i: (0, i, 0, 0))
    shape = jax.ShapeDtypeStruct((PEER_HEADS, m // tb, PEER_KEYS, tb), F32)
    _, pos_list, cand_rows = _pair_candidates()
    pos = np.full((cand_rows, tb), PEER_TOPK * PEER_TOPK, np.int32)
    pos[:len(pos_list)] = np.asarray(pos_list, np.int32)[:, None]
    return pl.pallas_call(
        _peer_retrieve_kernel,
        grid=(m // tb,),
        in_specs=[pl.BlockSpec((qt.shape[0], tb), lambda i: (0, i)),
                  pl.BlockSpec((nhj, PEER_KEYS, PEER_HALF), lambda i: (0, 0, 0)),
                  pl.BlockSpec((cand_rows, tb), lambda i: (0, 0))],
        out_specs=[spec] * 7,
        out_shape=[shape] * 7,
        compiler_params=_params(("parallel",)),
        name="peer_retrieve",
    )(qt, keys16, jnp.asarray(pos))


def _gelu(x):
    return 0.5 * x * (1.0 + lax.erf(x * (2.0 ** -0.5)))


def _peer_expert_kernel(xnt_ref, s1_ref, c1_ref, thr1_ref, eq1_ref, s2_ref, e2_ref, thr2_ref,
                        u_ref, vt_ref, o_ref, act0, act1, hd, *, n_eblk, a_per_blk):
    e = pl.program_id(1)
    nk = PEER_KEYS
    tt = xnt_ref.shape[1]

    def u_dot(dst):
        dst[...] = _dot(u_ref[...], xnt_ref[...])

    def mask_gelu(src, ai):
        a = (e - 1) * a_per_blk + ai
        row = pl.ds(a, 1)
        for tl in range(tt // LANES):
            ls = slice(tl * LANES, (tl + 1) * LANES)
            w = jnp.zeros((nk, LANES), F32)
            for h in range(PEER_HEADS):
                thr = jnp.where(eq1_ref[h, tl, row, :] > 0.5, thr2_ref[h, tl], thr1_ref[h, tl, row, :])
                pair = s2_ref[h, tl] + s1_ref[h, tl, row, :]
                w = w + jnp.where(pair >= thr, e2_ref[h, tl] * c1_ref[h, tl, row, :], 0.0)
            hd[ai * nk:(ai + 1) * nk, ls] = (_gelu(src[ai * nk:(ai + 1) * nk, ls]) * w).astype(BF16)

    def skewed(dst, src):
        u_dot(dst)
        for ai in range(a_per_blk):
            mask_gelu(src, ai)
        o_ref[...] += _dot(vt_ref[...], hd[...])

    @pl.when(e == 0)
    def _():
        o_ref[...] = jnp.zeros(o_ref.shape, F32)
        u_dot(act0)

    @pl.when((e > 0) & (e < n_eblk) & (e % 2 == 1))
    def _():
        skewed(act1, act0)

    @pl.when((e > 0) & (e < n_eblk) & (e % 2 == 0))
    def _():
        skewed(act0, act1)

    @pl.when(e == n_eblk)
    def _():
        src = act1 if (n_eblk - 1) % 2 == 1 else act0
        for ai in range(a_per_blk):
            mask_gelu(src, ai)
        o_ref[...] += _dot(vt_ref[...], hd[...])


def _peer_experts(xnt, tiles, u16, vt16, *, tt, eb):
    d, m = xnt.shape
    n_exp = u16.shape[0]
    n_eblk = n_exp // eb
    once = pl.Buffered(1)
    tile_spec = pl.BlockSpec((PEER_HEADS, tt // LANES, PEER_KEYS, LANES), lambda i, e: (0, i, 0, 0),
                             pipeline_mode=once)
    return pl.pallas_call(
        functools.partial(_peer_expert_kernel, n_eblk=n_eblk, a_per_blk=eb // PEER_KEYS),
        grid=(m // tt, n_eblk + 1),
        in_specs=[pl.BlockSpec((d, tt), lambda i, e: (0, i), pipeline_mode=once)]
        + [tile_spec] * 7
        + [pl.BlockSpec((eb, d), lambda i, e: (jnp.minimum(e, n_eblk - 1), 0)),
           pl.BlockSpec((d, eb), lambda i, e: (0, jnp.maximum(e - 1, 0)))],
        out_specs=pl.BlockSpec((d, tt), lambda i, e: (0, i), pipeline_mode=once),
        out_shape=jax.ShapeDtypeStruct((d, m), F32),
        scratch_shapes=[pltpu.VMEM((eb, tt), F32), pltpu.VMEM((eb, tt), F32), pltpu.VMEM((eb, tt), BF16)],
        compiler_params=_params(("parallel", "arbitrary")),
        name="peer_experts",
    )(xnt, *tiles, u16, vt16)


def _final_kernel(h_ref, ft_ref, g_ref, o_ref):
    x = h_ref[...] + ft_ref[...].T
    ms = jnp.mean(x * x, axis=-1, keepdims=True)
    o_ref[...] = x * lax.rsqrt(ms + NORM_EPS) * g_ref[...]


def _final_norm(h, ffn_t, g, *, tm):
    m, d = h.shape
    row = pl.BlockSpec((tm, d), lambda i: (i, 0))
    return pl.pallas_call(
        _final_kernel,
        grid=(m // tm,),
        in_specs=[row, pl.BlockSpec((d, tm), lambda i: (0, i)), pl.BlockSpec((1, d), lambda i: (0, 0))],
        out_specs=row,
        out_shape=jax.ShapeDtypeStruct((m, d), F32),
        compiler_params=_params(("parallel",)),
        name="final_norm",
    )(h, ffn_t, g)


def _tile(m, pref):
    return pref if m % pref == 0 else m


def _in_proj(x, g_mix, w16, wdt16, *, transposed):
    m = x.shape[0]
    tm = _tile(m, 512)
    qw = 2 * ATTN_HEADS * ATTN_HEAD_DIM
    t16 = "bf16T" if transposed else "bf16"
    segs = [(0, qw, ATTN_SCALE, (t16,)), (qw, qw, 1.0, ("f32", "bf16")), (2 * qw, ATTN_WIDTH, 1.0, ("f32", t16)),
            (2 * qw + ATTN_WIDTH, SSM_WIDTH, 1.0, ("f32",)),
            (2 * qw + ATTN_WIDTH + SSM_WIDTH, SSM_CONV_DIM, 1.0, ("f32",))]
    q16, k32, k16, v32, v16, z32, xbc32, xn16 = _norm_mm(x, g_mix, w16, segs, tm=tm, tn=512, emit_xn=True)
    dt_raw = _mm([xn16], [wdt16], None, tm=tm, tn=LANES)
    return q16, k32, k16, v32, v16, z32, xbc32, dt_raw


def _peer_and_final(h, g_ffn, wqt16, keys16, u16, vt16, g_final):
    m = h.shape[0]
    xnt, qt = _peer_query(h, g_ffn, wqt16, tm=_tile(m, 256), tr=512)
    tiles = _peer_retrieve(qt, keys16)
    ffn_t = _peer_experts(xnt, tiles, u16, vt16, tt=_tile(m, 512), eb=512)
    return _final_norm(h, ffn_t, g_final, tm=_tile(m, 256))


def kernel(x_prompt, x_sample, cache_k, cache_v, state_conv, state_ssm, page_table, g_mix, w_in, lambda_q, lambda_k, subln_g, conv_w, conv_b, dt_bias, a_log, d_skip, ssm_norm_g, w_out, g_ffn, peer_wq, peer_keys, peer_u, peer_v, g_final):
    assert w_in.shape[0] == 1, "single trunk layer"
    bsz, seq, dm = x_prompt.shape
    nb = x_sample.shape[0]
    assert bsz == 1 and x_sample.shape[1] == 1
    main_cols = w_in.shape[2] - SSM_HEADS
    w16 = w_in[0].astype(BF16)
    wdt16 = jnp.pad(w_in[0][:, main_cols:], ((0, 0), (0, LANES - SSM_HEADS))).astype(BF16)
    wo16 = w_out[0].astype(BF16)
    wqt16 = peer_wq[0].T.astype(BF16)
    keys16 = peer_keys[0].reshape(2 * PEER_HEADS, PEER_KEYS, PEER_HALF).astype(BF16)
    u16 = peer_u[0].astype(BF16)
    vt16 = peer_v[0].T.astype(BF16)
    gm = g_mix[0].reshape(1, -1)
    gf = g_ffn[0].reshape(1, -1)
    gfin = g_final.reshape(1, -1)
    sg = subln_g[0].reshape(1, -1)
    lq, lk = lambda_q[0], lambda_k[0]

    xp = x_prompt.reshape(seq, dm)
    qt16, k32, k16, v32, vtok16, z32, xbc32, dtr = _in_proj(xp, gm, w16, wdt16, transposed=True)
    attn16 = _prompt_attention(qt16, k16, vtok16, lq, lk, sg, t=_tile(seq, 512)).T
    y16, hfin = _ssd_prompt(xbc32, z32, dtr, conv_w[0], conv_b[0], dt_bias[0], a_log[0], d_skip[0],
                            ssm_norm_g[0])
    hp = _mm([attn16, y16], [wo16[:ATTN_WIDTH], wo16[ATTN_WIDTH:]], xp, tm=_tile(seq, 512), tn=512)
    y_prompt = _peer_and_final(hp, gf, wqt16, keys16, u16, vt16, gfin)

    xs = x_sample.reshape(nb, dm)
    sq16, sk32, _, sv32, _, sz32, sxbc32, sdtr = _in_proj(xs, gm, w16, wdt16, transposed=False)
    sattn = _sample_attention(sq16.astype(F32), sk32, sv32, cache_k[0], cache_v[0], page_table, lq, lk, sg)
    sy, conv_s, ssm_s = _ssd_sample(sxbc32, sz32, sdtr, state_conv[0], state_ssm[0], conv_w[0], conv_b[0],
                                    dt_bias[0], a_log[0], d_skip[0], ssm_norm_g[0])
    hs = _mm([sattn.astype(BF16), sy.astype(BF16)], [wo16[:ATTN_WIDTH], wo16[ATTN_WIDTH:]], xs,
             tm=nb, tn=512)
    y_sample = _peer_and_final(hs, gf, wqt16, keys16, u16, vt16, gfin)

    return (y_prompt.reshape(bsz, seq, dm),
            y_sample.reshape(nb, 1, dm),
            k32.reshape(1, bsz, seq, ATTN_HEADS, 2, ATTN_HEAD_DIM),
            v32.reshape(1, bsz, seq, ATTN_HEADS, ATTN_V_DIM),
            xbc32[seq - (SSM_CONV - 1):].reshape(1, bsz, SSM_CONV - 1, SSM_CONV_DIM),
            hfin.reshape(1, bsz, SSM_HEADS, SSM_HEAD_DIM, SSM_STATE),
            sk32.reshape(1, nb, 1, ATTN_HEADS, 2, ATTN_HEAD_DIM),
            sv32.reshape(1, nb, 1, ATTN_HEADS, ATTN_V_DIM),
            conv_s.reshape(1, nb, SSM_CONV - 1, SSM_CONV_DIM),
            ssm_s.reshape(1, nb, SSM_HEADS, SSM_HEAD_DIM, SSM_STATE))
```
